```python
import jax
import jax.numpy as jnp
from jax import lax
import numpy as np

D_MODEL = 4096
BATCH = 1
SEQ = 16384
DEPTH = 4

GRID_W = 64
CTX_LEN = 256
N_MOD = 9
ADA_RANK = 256
D_FF = 6144
FFN_RES = 0.5
EPS = 1e-6
D_A = D_MODEL // 2
HEAD_A = 64
H_A = D_A // HEAD_A
R_DECAY = 128
R_AAA = 128
R_GATE = 256
RWKV_LN_EPS = 64e-5
D_B_K = D_MODEL // 4
D_B_V = D_MODEL // 2
H_B = 4
DK_B = D_B_K // H_B
DV_B = D_B_V // H_B
R_GLA_GATE = 16
GLA_GATE_NORM = 16.0
GLA_CHUNK = 64
D_AB_IN = 3 * D_A + 2 * D_B_K + 2 * D_B_V
HEAD_C = 128
H_C = D_MODEL // HEAD_C
KV_C = 8
G_C = H_C // KV_C
KV_DIM = KV_C * HEAD_C
ROPE_PAIRS = HEAD_C // 4
ROPE_THETA = 10000.0
Q_BLOCK = 128

kernel_name = 'hybrid_rwkv7_gla_gqa_prefix_dit'


def rmsnorm(x, gain):
    xf = x.astype(jnp.float32)
    y = xf * lax.rsqrt(jnp.mean(xf * xf, axis=-1, keepdims=True) + EPS)
    return (y * gain.astype(jnp.float32)).astype(x.dtype)


def modulation(cvec, down, up, bias):
    m = (jax.nn.silu(cvec) @ down) @ up + bias
    return jnp.split(m[:, None, :], N_MOD, axis=-1)


def modulate(h, shift, scale):
    return h * (1.0 + scale) + shift


def swiglu(h, w13, w2):
    gate, up = jnp.split(h @ w13, 2, axis=-1)
    return (jax.nn.silu(gate) * up) @ w2


def ffn_half(h, mods, k, gain, w13, w2):
    u = modulate(rmsnorm(h, gain), mods[3 * k], mods[3 * k + 1])
    return h + FFN_RES * mods[3 * k + 2] * swiglu(u, w13, w2)


def centred_mix(x, mu):
    prev = jnp.pad(x, ((0, 0), (1, 0), (0, 0)))[:, :-1]
    nxt = jnp.pad(x, ((0, 0), (0, 1), (0, 0)))[:, 1:]
    return x + mu[0] * (prev - x) + mu[1] * (nxt - x)


def rwkv_prep(u, p_rkv, mu_rkv, mu_lora, w1, w2, w0, a1, a2, a0, g1, g2, k_k, k_a):
    r, k, v = jnp.split(centred_mix(p_rkv, mu_rkv), 3, axis=-1)
    xw = centred_mix(u, mu_lora[0])
    xa = centred_mix(u, mu_lora[1])
    xg = centred_mix(u, mu_lora[2])
    lw = jnp.einsum('dbtr,dro->dbto', jnp.tanh(jnp.einsum('btc,dcr->dbtr', xw, w1)), w2) + w0[:, None, None, :]
    w_raw = -jax.nn.softplus(-lw.astype(jnp.float32)) - 0.5
    decay = jnp.exp(-jnp.exp(w_raw))
    za = jnp.einsum('dbtr,dro->dbto', jnp.einsum('btc,dcr->dbtr', xa, a1), a2) + a0[:, None, None, :]
    a = jax.nn.sigmoid(za.astype(jnp.float32))
    g = jax.nn.sigmoid(xg @ g1) @ g2
    heads = lambda t: t.reshape(*t.shape[:-1], H_A, HEAD_A)
    kf = k.astype(jnp.float32)
    kk = heads(kf * k_k)
    kk = kk / jnp.maximum(jnp.sqrt(jnp.sum(kk * kk, axis=-1, keepdims=True)), 1e-12)
    k_dir = kf[None] * (1.0 + (a - 1.0) * k_a)
    return (heads(r.astype(jnp.float32)), heads(decay), heads(k_dir), heads(v.astype(jnp.float32)), kk, heads(a), g)


def rwkv7_scan(r, w, k, v, kk, a, s0, reverse):
    def step(S, inp):
        r_t, w_t, k_t, v_t, kk_t, a_t = inp
        sa = jnp.einsum('bhvk,bhk->bhv', S, kk_t)
        S = S * w_t[:, :, None, :] - sa[..., None] * (kk_t * a_t)[:, :, None, :] + v_t[..., None] * k_t[:, :, None, :]
        return S, jnp.einsum('bhvk,bhk->bhv', S, r_t)
    xs = tuple(jnp.moveaxis(t, 1, 0) for t in (r, w, k, v, kk, a))
    s_fin, y = lax.scan(step, s0, xs, reverse=reverse)
    return jnp.moveaxis(y, 0, 1), s_fin


def rwkv_bidir(pl, pc):
    r_l, w_l, k_l, v_l, kk_l, a_l, _ = pl
    r_c, w_c, k_c, v_c, kk_c, a_c, _ = pc
    s0 = jnp.zeros((r_l.shape[0], H_A, HEAD_A, HEAD_A), jnp.float32)
    outs_l, outs_c = [], []
    for d, rev in ((0, False), (1, True)):
        yc, s_c = rwkv7_scan(r_c, w_c[d], k_c[d], v_c, kk_c, a_c[d], s0, rev)
        yl, _ = rwkv7_scan(r_l, w_l[d], k_l[d], v_l, kk_l, a_l[d], s_c, rev)
        outs_l.append(yl)
        outs_c.append(yc)
    return outs_l[0] + outs_l[1], outs_c[0] + outs_c[1]


def rwkv_out(y, prep, r_k, ln_w, ln_b, dtype):
    r, _, k_dir, v, _, _, g = prep
    B, T = y.shape[:2]
    mu = jnp.mean(y, axis=-1, keepdims=True)
    var = jnp.mean((y - mu) ** 2, axis=-1, keepdims=True)
    yn = ((y - mu) * lax.rsqrt(var + RWKV_LN_EPS)).reshape(B, T, D_A) * ln_w + ln_b
    bonus = (jnp.sum(r * k_dir * r_k, axis=-1, keepdims=True).sum(0) * v).reshape(B, T, D_A)
    return ((yn + bonus) * g).astype(dtype)


def gla_prep(u, p_gla, a1, a2, ab):
    q, k, v, go = jnp.split(p_gla, [D_B_K, 2 * D_B_K, 2 * D_B_K + D_B_V], axis=-1)
    z = jnp.einsum('dbtr,dro->dbto', jnp.einsum('btc,dcr->dbtr', u, a1), a2) + ab[:, None, None, :]
    log_a = jax.nn.log_sigmoid(z.astype(jnp.float32)) / GLA_GATE_NORM
    hk = lambda t: t.reshape(*t.shape[:-1], H_B, DK_B)
    vh = v.astype(jnp.float32).reshape(*v.shape[:-1], H_B, DV_B)
    return (hk(q.astype(jnp.float32) * DK_B ** -0.5), hk(k.astype(jnp.float32)), vh, hk(log_a), go)


def gla_chunked(q, k, v, log_a, s0):
    B, T, H, _ = q.shape
    n = T // GLA_CHUNK
    chunks = lambda t: t.reshape(B, n, GLA_CHUNK, H, t.shape[-1]).transpose(1, 0, 3, 2, 4)
    causal = jnp.tril(jnp.ones((GLA_CHUNK, GLA_CHUNK), bool))[None, None, :, :, None]
    def step(S, inp):
        qc, kc, vc, gc = inp
        b = jnp.cumsum(gc, axis=2)
        b_last = b[:, :, -1:, :]
        o_inter = jnp.einsum('bhtk,bhkv->bhtv', qc * jnp.exp(b), S)
        diff = b[:, :, :, None, :] - b[:, :, None, :, :]
        decay = jnp.exp(jnp.where(causal, diff, -jnp.inf))
        att = jnp.einsum('bhtk,bhsk,bhtsk->bhts', qc, kc, decay)
        o_intra = jnp.einsum('bhts,bhsv->bhtv', att, vc)
        S = jnp.exp(b_last[:, :, 0, :, None]) * S + jnp.einsum('bhsk,bhsv->bhkv', kc * jnp.exp(b_last - b), vc)
        return S, o_inter + o_intra
    s_fin, o = lax.scan(step, s0, (chunks(q), chunks(k), chunks(v), chunks(log_a)))
    return o.transpose(1, 0, 3, 2, 4).reshape(B, T, H, v.shape[-1]), s_fin


def gla_bidir(pl, pc):
    q_l, k_l, v_l, la_l, _ = pl
    q_c, k_c, v_c, la_c, _ = pc
    fl = lambda t: jnp.flip(t, axis=1)
    s0 = jnp.zeros((q_l.shape[0], H_B, DK_B, DV_B), jnp.float32)
    o_c_f, s_cf = gla_chunked(q_c, k_c, v_c, la_c[0], s0)
    o_l_f, _ = gla_chunked(q_l, k_l, v_l, la_l[0], s_cf)
    o_c_b, s_cb = gla_chunked(fl(q_c), fl(k_c), fl(v_c), fl(la_c[1]), s0)
    o_l_b, _ = gla_chunked(fl(q_l), fl(k_l), fl(v_l), fl(la_l[1]), s_cb)
    return o_l_f + fl(o_l_b), o_c_f + fl(o_c_b)


def gla_out(o, go, gain, dtype):
    B, T = o.shape[:2]
    return (rmsnorm(o, gain).reshape(B, T, D_B_V) * jax.nn.silu(go.astype(jnp.float32))).astype(dtype)


def mixer_ab(u_lat, u_ctx, w_in, w_out, mu_rkv, mu_lora, w1, w2, w0, a1, a2, a0, g1, g2, k_k, k_a, r_k, ln_w, ln_b, ga1, ga2, gab, g_norm, need_ctx):
    p_lat = u_lat @ w_in
    p_ctx = u_ctx @ w_in
    rw = (mu_rkv, mu_lora, w1, w2, w0, a1, a2, a0, g1, g2, k_k, k_a)
    ra_l = rwkv_prep(u_lat, p_lat[..., :3 * D_A], *rw)
    ra_c = rwkv_prep(u_ctx, p_ctx[..., :3 * D_A], *rw)
    gb_l = gla_prep(u_lat, p_lat[..., 3 * D_A:], ga1, ga2, gab)
    gb_c = gla_prep(u_ctx, p_ctx[..., 3 * D_A:], ga1, ga2, gab)
    ya_l, ya_c = rwkv_bidir(ra_l, ra_c)
    yb_l, yb_c = gla_bidir(gb_l, gb_c)
    y_lat = jnp.concatenate([rwkv_out(ya_l, ra_l, r_k, ln_w, ln_b, u_lat.dtype), gla_out(yb_l, gb_l[4], g_norm, u_lat.dtype)], axis=-1) @ w_out
    y_ctx = None
    if need_ctx:
        y_ctx = jnp.concatenate([rwkv_out(ya_c, ra_c, r_k, ln_w, ln_b, u_ctx.dtype), gla_out(yb_c, gb_c[4], g_norm, u_ctx.dtype)], axis=-1) @ w_out
    return y_lat, y_ctx


def axial_rope(x, ang):
    B, T, H, _ = x.shape
    xr = x.astype(jnp.float32).reshape(B, T, H, 2, 2, ROPE_PAIRS)
    cos = jnp.cos(ang)[None, :, None, :, :]
    sin = jnp.sin(ang)[None, :, None, :, :]
    x1, x2 = xr[..., 0, :], xr[..., 1, :]
    out = jnp.stack([x1 * cos - x2 * sin, x2 * cos + x1 * sin], axis=-2)
    return out.reshape(B, T, H, HEAD_C).astype(x.dtype)


def gqa_attend(q, k, v):
    B, Lq = q.shape[:2]
    qg = q.reshape(B, Lq, KV_C, G_C, HEAD_C)
    s = jnp.einsum('bqngd,bknd->bngqk', qg, k).astype(jnp.float32) * (HEAD_C ** -0.5)
    p = jax.nn.softmax(s, axis=-1).astype(v.dtype)
    return jnp.einsum('bngqk,bknd->bqngd', p, v).reshape(B, Lq, H_C * HEAD_C)


def mixer_c(u_lat, u_ctx, w_in, w_out, q_gain, k_gain, need_ctx):
    B, T, _ = u_lat.shape
    def qkv(u):
        p = u @ w_in
        q = p[..., :D_MODEL].reshape(B, -1, H_C, HEAD_C)
        k = p[..., D_MODEL:D_MODEL + KV_DIM].reshape(B, -1, KV_C, HEAD_C)
        v = p[..., D_MODEL + KV_DIM:].reshape(B, -1, KV_C, HEAD_C)
        return rmsnorm(q, q_gain), rmsnorm(k, k_gain), v
    q_l, k_l, v_l = qkv(u_lat)
    q_c, k_c, v_c = qkv(u_ctx)
    rows = T // GRID_W
    row = jnp.repeat(jnp.arange(rows), GRID_W)
    col = jnp.tile(jnp.arange(GRID_W), rows)
    inv_freq = ROPE_THETA ** (-jnp.arange(ROPE_PAIRS, dtype=jnp.float32) / ROPE_PAIRS)
    ang = jnp.stack([row, col], axis=-1).astype(jnp.float32)[:, :, None] * inv_freq
    q_l = axial_rope(q_l, ang)
    k_l = axial_rope(k_l, ang)
    k_all = jnp.concatenate([k_c, k_l], axis=1)
    v_all = jnp.concatenate([v_c, v_l], axis=1)
    nb = T // Q_BLOCK
    qb = q_l.reshape(B, nb, Q_BLOCK, H_C, HEAD_C).transpose(1, 0, 2, 3, 4)
    o = lax.map(lambda qi: gqa_attend(qi, k_all, v_all), qb)
    y_lat = o.transpose(1, 0, 2, 3).reshape(B, T, D_MODEL) @ w_out
    y_ctx = None
    if need_ctx:
        y_ctx = gqa_attend(q_c, k_c, v_c) @ w_out
    return y_lat, y_ctx


def setup_inputs(seed: int = 0) -> dict:
    key = jax.random.key(seed)
    ks = iter(jax.random.split(key, 64))
    D = D_MODEL
    NE = (DEPTH + 1) // 2
    NO = DEPTH // 2
    def normal(shape, scale):
        return jax.random.normal(next(ks), shape, jnp.float32) * scale
    def unif(shape, lo, hi):
        return jax.random.uniform(next(ks), shape, jnp.float32, lo, hi)
    def gain(shape):
        return 1.0 + normal(shape, 0.05)
    return {
        'x': normal((BATCH, SEQ, D), 1.0),
        'c': normal((BATCH, D), 1.0),
        'ctx': normal((BATCH, CTX_LEN, D), 1.0),
        'c_ctx': normal((D,), 1.0),
        'ada_down': normal((DEPTH, D, ADA_RANK), D ** -0.5),
        'ada_up': normal((DEPTH, ADA_RANK, N_MOD * D), 0.5 * ADA_RANK ** -0.5),
        'ada_bias': normal((DEPTH, N_MOD * D), 0.02),
        'norm_gains': gain((DEPTH, 3, D)),
        'final_gain': gain((D,)),
        'ffn_w13': normal((DEPTH, 2, D, 2 * D_FF), D ** -0.5),
        'ffn_w2': normal((DEPTH, 2, D_FF, D), D_FF ** -0.5),
        'ab_w_in': normal((NE, D, D_AB_IN), D ** -0.5),
        'ab_w_out': normal((NE, D, D), D ** -0.5),
        'rwkv_mu_rkv': unif((NE, 2, 3 * D_A), 0.0, 0.5),
        'rwkv_mu_lora': unif((NE, 3, 2, D), 0.0, 0.5),
        'rwkv_w1': normal((NE, 2, D, R_DECAY), D ** -0.5),
        'rwkv_w2': normal((NE, 2, R_DECAY, D_A), 0.5 * R_DECAY ** -0.5),
        'rwkv_w0': unif((NE, 2, D_A), -6.0, -1.0),
        'rwkv_a1': normal((NE, 2, D, R_AAA), D ** -0.5),
        'rwkv_a2': normal((NE, 2, R_AAA, D_A), 0.5 * R_AAA ** -0.5),
        'rwkv_a0': normal((NE, 2, D_A), 0.5),
        'rwkv_g1': normal((NE, D, R_GATE), D ** -0.5),
        'rwkv_g2': normal((NE, R_GATE, D_A), R_GATE ** -0.5),
        'rwkv_k_k': 0.85 + normal((NE, D_A), 0.05),
        'rwkv_k_a': gain((NE, D_A)),
        'rwkv_r_k': normal((NE, H_A, HEAD_A), 0.1),
        'rwkv_ln_w': gain((NE, D_A)),
        'rwkv_ln_b': normal((NE, D_A), 0.02),
        'gla_a1': normal((NE, 2, D, R_GLA_GATE), D ** -0.5),
        'gla_a2': normal((NE, 2, R_GLA_GATE, D_B_K), R_GLA_GATE ** -0.5),
        'gla_ab': normal((NE, 2, D_B_K), 0.5),
        'gla_norm': gain((NE, DV_B)),
        'attn_w_in': normal((NO, D, D + 2 * KV_DIM), D ** -0.5),
        'attn_w_out': normal((NO, D, D), D ** -0.5),
        'attn_q_norm': gain((NO, HEAD_C)),
        'attn_k_norm': gain((NO, HEAD_C)),
    }


def reference(x, c, ctx, c_ctx, ada_down, ada_up, ada_bias, norm_gains, final_gain, ffn_w13, ffn_w2, ab_w_in, ab_w_out, rwkv_mu_rkv, rwkv_mu_lora, rwkv_w1, rwkv_w2, rwkv_w0, rwkv_a1, rwkv_a2, rwkv_a0, rwkv_g1, rwkv_g2, rwkv_k_k, rwkv_k_a, rwkv_r_k, rwkv_ln_w, rwkv_ln_b, gla_a1, gla_a2, gla_ab, gla_norm, attn_w_in, attn_w_out, attn_q_norm, attn_k_norm):
    h_lat, h_ctx = x, ctx
    for layer in range(DEPTH):
        last = layer == DEPTH - 1
        i = layer // 2
        m_l = modulation(c, ada_down[layer], ada_up[layer], ada_bias[layer])
        m_c = modulation(c_ctx[None, :], ada_down[layer], ada_up[layer], ada_bias[layer])
        gains = norm_gains[layer]
        h_lat = ffn_half(h_lat, m_l, 0, gains[0], ffn_w13[layer, 0], ffn_w2[layer, 0])
        h_ctx = ffn_half(h_ctx, m_c, 0, gains[0], ffn_w13[layer, 0], ffn_w2[layer, 0])
        u_lat = modulate(rmsnorm(h_lat, gains[1]), m_l[3], m_l[4])
        u_ctx = modulate(rmsnorm(h_ctx, gains[1]), m_c[3], m_c[4])
        if layer % 2 == 0:
            y_lat, y_ctx = mixer_ab(u_lat, u_ctx, ab_w_in[i], ab_w_out[i], rwkv_mu_rkv[i], rwkv_mu_lora[i], rwkv_w1[i], rwkv_w2[i], rwkv_w0[i], rwkv_a1[i], rwkv_a2[i], rwkv_a0[i], rwkv_g1[i], rwkv_g2[i], rwkv_k_k[i], rwkv_k_a[i], rwkv_r_k[i], rwkv_ln_w[i], rwkv_ln_b[i], gla_a1[i], gla_a2[i], gla_ab[i], gla_norm[i], not last)
        else:
            y_lat, y_ctx = mixer_c(u_lat, u_ctx, attn_w_in[i], attn_w_out[i], attn_q_norm[i], attn_k_norm[i], not last)
        h_lat = h_lat + m_l[5] * y_lat
        h_lat = ffn_half(h_lat, m_l, 2, gains[2], ffn_w13[layer, 1], ffn_w2[layer, 1])
        if not last:
            h_ctx = h_ctx + m_c[5] * y_ctx
            h_ctx = ffn_half(h_ctx, m_c, 2, gains[2], ffn_w13[layer, 1], ffn_w2[layer, 1])
    return rmsnorm(h_lat, final_gain)
```

```python
import functools

import jax
import jax.numpy as jnp
from jax import lax
from jax.experimental import pallas as pl
from jax.experimental.pallas import tpu as pltpu

F32 = jnp.float32
BF16 = jnp.bfloat16
HIGHEST = lax.Precision.HIGHEST

D_MODEL = 4096
DEPTH = 4
GRID_W = 64
N_MOD = 9
D_FF = 6144
FFN_RES = 0.5
EPS = 1e-6
D_A = D_MODEL // 2
HEAD_A = 64
H_A = D_A // HEAD_A
RWKV_LN_EPS = 64e-5
D_B_K = D_MODEL // 4
D_B_V = D_MODEL // 2
H_B = 4
DK_B = D_B_K // H_B
DV_B = D_B_V // H_B
GLA_GATE_NORM = 16.0
HEAD_C = 128
H_C = D_MODEL // HEAD_C
KV_C = 8
G_C = H_C // KV_C
KV_DIM = KV_C * HEAD_C
ROPE_PAIRS = HEAD_C // 4
ROPE_THETA = 10000.0

LANES = 128
SCAN_CHUNK = 64
VMEM_LIMIT = 56 * 1024 * 1024


def _pick(n, prefs):
    for p in prefs:
        if n % p == 0:
            return p
    return n


def _cparams(sem):
    return pltpu.CompilerParams(dimension_semantics=sem, vmem_limit_bytes=VMEM_LIMIT)


def _mm_kernel(x_ref, w_ref, o_ref):
    o_ref[...] = jnp.dot(x_ref[...], w_ref[...], preferred_element_type=F32).astype(o_ref.dtype)


def _mm_swiglu_kernel(x_ref, wg_ref, wu_ref, o_ref):
    x = x_ref[...]
    g = jnp.dot(x, wg_ref[...], preferred_element_type=F32)
    u = jnp.dot(x, wu_ref[...], preferred_element_type=F32)
    o_ref[...] = (g / (1.0 + jnp.exp(-g)) * u).astype(o_ref.dtype)


def _mm_res_kernel(x_ref, w_ref, h_ref, c_ref, o_ref):
    y = jnp.dot(x_ref[...], w_ref[...], preferred_element_type=F32)
    o_ref[...] = h_ref[...] + c_ref[...] * y


def _mm_tiles(M, K, N):
    tm = _pick(M, (1024, 512, 256, 128, 64, 32, 16))
    tn = _pick(N, (1024, 512, 256, 128))
    if K > 4096:
        tn = min(tn, 512)
    return tm, tn


def matmul(x, w, out_dtype=F32):
    M, K = x.shape
    N = w.shape[1]
    tm, tn = _mm_tiles(M, K, N)
    return pl.pallas_call(
        _mm_kernel,
        grid=(M // tm, N // tn),
        in_specs=[pl.BlockSpec((tm, K), lambda i, j: (i, 0)),
                  pl.BlockSpec((K, tn), lambda i, j: (0, j))],
        out_specs=pl.BlockSpec((tm, tn), lambda i, j: (i, j)),
        out_shape=jax.ShapeDtypeStruct((M, N), out_dtype),
        compiler_params=_cparams(("parallel", "parallel")),
    )(x, w)


def matmul_swiglu(x, w13):
    M, K = x.shape
    F = w13.shape[1] // 2
    tm, _ = _mm_tiles(M, K, F)
    tn = _pick(F, (512, 256, 128))
    nf = F // tn
    return pl.pallas_call(
        _mm_swiglu_kernel,
        grid=(M // tm, nf),
        in_specs=[pl.BlockSpec((tm, K), lambda i, j: (i, 0)),
                  pl.BlockSpec((K, tn), lambda i, j: (0, j)),
                  pl.BlockSpec((K, tn), lambda i, j: (0, j + nf))],
        out_specs=pl.BlockSpec((tm, tn), lambda i, j: (i, j)),
        out_shape=jax.ShapeDtypeStruct((M, F), BF16),
        compiler_params=_cparams(("parallel", "parallel")),
    )(x, w13, w13)


def matmul_residual(x, w, h, coef):
    M, K = x.shape
    N = w.shape[1]
    tm, tn = _mm_tiles(M, K, N)
    tn = min(tn, 512)
    return pl.pallas_call(
        _mm_res_kernel,
        grid=(M // tm, N // tn),
        in_specs=[pl.BlockSpec((tm, K), lambda i, j: (i, 0)),
                  pl.BlockSpec((K, tn), lambda i, j: (0, j)),
                  pl.BlockSpec((tm, tn), lambda i, j: (i, j)),
                  pl.BlockSpec((1, tn), lambda i, j: (0, j))],
        out_specs=pl.BlockSpec((tm, tn), lambda i, j: (i, j)),
        out_shape=jax.ShapeDtypeStruct((M, N), F32),
        compiler_params=_cparams(("parallel", "parallel")),
    )(x, w, h, coef)


def _normmod_kernel(h_ref, g_ref, s_ref, o_ref):
    x = h_ref[...]
    y = x * lax.rsqrt(jnp.mean(x * x, axis=-1, keepdims=True) + EPS)
    o_ref[...] = (y * g_ref[...] + s_ref[...]).astype(o_ref.dtype)


def normmod(h, g, s, out_dtype):
    M, D = h.shape
    tm = _pick(M, (256, 128, 64, 32, 16, 8))
    return pl.pallas_call(
        _normmod_kernel,
        grid=(M // tm,),
        in_specs=[pl.BlockSpec((tm, D), lambda i: (i, 0)),
                  pl.BlockSpec((1, D), lambda i: (0, 0)),
                  pl.BlockSpec((1, D), lambda i: (0, 0))],
        out_specs=pl.BlockSpec((tm, D), lambda i: (i, 0)),
        out_shape=jax.ShapeDtypeStruct((M, D), out_dtype),
        compiler_params=_cparams(("parallel",)),
    )(h, g, s)


def _flash_kernel(q_ref, k_ref, v_ref, o_ref, m_sc, l_sc, acc_sc):
    j = pl.program_id(2)

    @pl.when(j == 0)
    def _():
        m_sc[...] = jnp.full(m_sc.shape, -1e30, F32)
        l_sc[...] = jnp.zeros(l_sc.shape, F32)
        acc_sc[...] = jnp.zeros(acc_sc.shape, F32)

    k = k_ref[...]
    v = v_ref[...]
    for g in range(G_C):
        q = q_ref[:, g * HEAD_C:(g + 1) * HEAD_C]
        s = lax.dot_general(q, k, (((1,), (1,)), ((), ())), preferred_element_type=F32)
        m_prev = m_sc[g]
        m_new = jnp.maximum(m_prev, jnp.max(s, axis=-1, keepdims=True))
        alpha = jnp.exp(m_prev - m_new)
        p = jnp.exp(s - m_new)
        l_sc[g] = alpha * l_sc[g] + jnp.sum(p, axis=-1, keepdims=True)
        acc_sc[g] = alpha * acc_sc[g] + jnp.dot(p.astype(BF16), v, preferred_element_type=F32)
        m_sc[g] = m_new

    @pl.when(j == pl.num_programs(2) - 1)
    def _():
        for g in range(G_C):
            o_ref[:, g * HEAD_C:(g + 1) * HEAD_C] = (acc_sc[g] / l_sc[g]).astype(o_ref.dtype)


def flash_gqa(q, k, v):
    Lq = q.shape[0]
    Lk = k.shape[0]
    tq = _pick(Lq, (512, 256, 128))
    tk = _pick(Lk, (1280, 1024, 640, 512, 256, 128))
    gw = G_C * HEAD_C
    return pl.pallas_call(
        _flash_kernel,
        grid=(KV_C, Lq // tq, Lk // tk),
        in_specs=[pl.BlockSpec((tq, gw), lambda n, i, j: (i, n)),
                  pl.BlockSpec((tk, HEAD_C), lambda n, i, j: (j, n)),
                  pl.BlockSpec((tk, HEAD_C), lambda n, i, j: (j, n))],
        out_specs=pl.BlockSpec((tq, gw), lambda n, i, j: (i, n)),
        out_shape=jax.ShapeDtypeStruct((Lq, H_C * HEAD_C), BF16),
        scratch_shapes=[pltpu.VMEM((G_C, tq, 1), F32),
                        pltpu.VMEM((G_C, tq, 1), F32),
                        pltpu.VMEM((G_C, tq, HEAD_C), F32)],
        compiler_params=_cparams(("parallel", "parallel", "arbitrary")),
    )(q, k, v)


def _nt(a, b):
    return lax.dot_general(a, b, (((1,), (1,)), ((), ())), preferred_element_type=F32)


def _tn(a, b):
    return lax.dot_general(a, b, (((0,), (0,)), ((), ())), preferred_element_type=F32)


def _mm(a, b):
    return jnp.dot(a, b, preferred_element_type=F32)


def _rwkv_kernel(r_ref, lw_ref, k_ref, v_ref, kk_ref, a_ref, y_ref, s_sc, *, npairs):
    C = SCAN_CHUNK
    C2 = 2 * C

    @pl.when(pl.program_id(1) == 0)
    def _():
        s_sc[...] = jnp.zeros(s_sc.shape, F32)

    row = lax.broadcasted_iota(jnp.int32, (C, C), 0)
    col = lax.broadcasted_iota(jnp.int32, (C, C), 1)
    tri_incl = (col <= row).astype(F32)
    row2 = lax.broadcasted_iota(jnp.int32, (C2, C2), 0)
    col2 = lax.broadcasted_iota(jnp.int32, (C2, C2), 1)
    same = (row2 // C) == (col2 // C)
    strict2 = same & (col2 < row2)
    incl2 = same & (col2 <= row2)
    eye2 = (row2 == col2).astype(F32)
    lane = lax.broadcasted_iota(jnp.int32, (C, LANES), 1)
    head0 = lane < HEAD_A

    def stack(x):
        return jnp.concatenate([jnp.where(head0, x, 0.0), jnp.where(head0, 0.0, x)], axis=0)

    for p in range(npairs):
        sl = slice(p * LANES, (p + 1) * LANES)
        r = r_ref[:, sl]
        lw = lw_ref[:, sl]
        k = k_ref[:, sl]
        v = v_ref[:, sl]
        kk = kk_ref[:, sl]
        a = a_ref[:, sl]
        S = s_sc[p]

        cl = jnp.dot(tri_incl, lw, precision=HIGHEST, preferred_element_type=F32)
        cl_last = cl[C - 1:C, :]
        b = a * kk
        p_inv = jnp.exp(-cl)
        tail = jnp.exp(cl_last - cl)
        kkd2 = stack(kk * jnp.exp(cl - lw)).astype(BF16)
        rd2 = stack(r * jnp.exp(cl)).astype(BF16)
        kinv2 = stack(k * p_inv).astype(BF16)
        binv2 = stack(b * p_inv).astype(BF16)
        kdec2 = stack(k * tail).astype(BF16)
        bdec2 = stack(b * tail).astype(BF16)
        v2 = stack(v)
        v2b = v2.astype(BF16)
        Sb = S.astype(BF16)

        lhs = jnp.concatenate([kkd2, rd2], axis=0)
        gk = _nt(lhs, kinv2)
        gb = _nt(lhs, binv2)
        a_kk = jnp.where(strict2, gk[:C2], 0.0)
        a_bk = jnp.where(strict2, gb[:C2], 0.0)
        b_rk = jnp.where(incl2, gk[C2:], 0.0)
        b_rb = jnp.where(incl2, gb[C2:], 0.0)

        n = (-a_bk).astype(BF16)
        tinv = eye2 + n.astype(F32)
        for _ in range(5):
            nf = _mm(n, n)
            n = nf.astype(BF16)
            tinv = tinv + _mm(n, tinv.astype(BF16))

        inter = _nt(lhs, Sb)
        rhs = inter[:C2] + _mm(a_kk.astype(BF16), v2b)
        u2 = _mm(tinv.astype(BF16), rhs.astype(BF16))
        u2b = u2.astype(BF16)
        y2 = inter[C2:] + _mm(b_rk.astype(BF16), v2b) - _mm(b_rb.astype(BF16), u2b)
        y_ref[:, sl] = y2[:C] + y2[C:]
        s_sc[p] = S * jnp.exp(cl_last) + _tn(v2b, kdec2) - _tn(u2b, bdec2)


def rwkv_scan(r, lw, k, v, kk, a):
    L = r.shape[0]
    C = SCAN_CHUNK
    gw = 1024
    npairs = gw // LANES
    spec = pl.BlockSpec((C, gw), lambda g, c: (c, g))
    return pl.pallas_call(
        functools.partial(_rwkv_kernel, npairs=npairs),
        grid=(D_A // gw, L // C),
        in_specs=[spec] * 6,
        out_specs=spec,
        out_shape=jax.ShapeDtypeStruct((L, D_A), F32),
        scratch_shapes=[pltpu.VMEM((npairs, LANES, LANES), F32)],
        compiler_params=_cparams(("parallel", "arbitrary")),
    )(r, lw, k, v, kk, a)


def _gla_kernel(q_ref, k_ref, v_ref, g_ref, o_ref, s_sc):
    C = SCAN_CHUNK

    @pl.when(pl.program_id(1) == 0)
    def _():
        s_sc[...] = jnp.zeros(s_sc.shape, F32)

    q = q_ref[...]
    k = k_ref[...]
    v = v_ref[...].astype(BF16)
    g = g_ref[...]
    St = s_sc[...]

    row = lax.broadcasted_iota(jnp.int32, (C, C), 0)
    col = lax.broadcasted_iota(jnp.int32, (C, C), 1)
    tri_incl = (col <= row).astype(F32)
    b = jnp.dot(tri_incl, g, precision=HIGHEST, preferred_element_type=F32)
    b_last = b[C - 1:C, :]

    o = _nt((q * jnp.exp(b)).astype(BF16), St.astype(BF16))

    att = jnp.where(row == col, jnp.sum(q * k, axis=-1, keepdims=True), 0.0)
    trow = lax.broadcasted_iota(jnp.int32, (C, 1), 0)
    n = C
    while n >= 2:
        half = n // 2
        same = (row // n) == (col // n)
        row_hi = (row % n) >= half
        col_hi = (col % n) >= half
        mc = jnp.where(same & row_hi & col_hi & (col <= row), 1.0, 0.0) \
            - jnp.where(same & (~row_hi) & (~col_hi) & (col > row), 1.0, 0.0)
        c = jnp.dot(mc, g, precision=HIGHEST, preferred_element_type=F32)
        e = jnp.exp(-jnp.abs(c))
        hi = (trow % n) >= half
        qt = jnp.where(hi, q * e, 0.0).astype(BF16)
        kt = jnp.where(hi, 0.0, k * e).astype(BF16)
        att = att + jnp.where(same, _nt(qt, kt), 0.0)
        n = half

    o_ref[...] = o + _mm(att.astype(BF16), v)
    kd = (k * jnp.exp(b_last - b)).astype(BF16)
    s_sc[...] = St * jnp.exp(b_last) + _tn(v, kd)


def gla_scan(q, k, v, g):
    L = q.shape[0]
    C = SCAN_CHUNK
    kspec = pl.BlockSpec((C, DK_B), lambda h, c: (c, h))
    vspec = pl.BlockSpec((C, DV_B), lambda h, c: (c, h))
    return pl.pallas_call(
        _gla_kernel,
        grid=(H_B, L // C),
        in_specs=[kspec, kspec, vspec, kspec],
        out_specs=vspec,
        out_shape=jax.ShapeDtypeStruct((L, D_B_V), F32),
        scratch_shapes=[pltpu.VMEM((DV_B, DK_B), F32)],
        compiler_params=_cparams(("parallel", "arbitrary")),
    )(q, k, v, g)


def _silu(x):
    return x * jax.nn.sigmoid(x)


def _centred_mix(x, mu):
    prev = jnp.pad(x, ((1, 0), (0, 0)))[:-1]
    nxt = jnp.pad(x, ((0, 1), (0, 0)))[1:]
    return x + mu[0] * (prev - x) + mu[1] * (nxt - x)


def _ffn_half(h, mods, kidx, gain, w13, w2):
    shift, scale, gate = mods[3 * kidx], mods[3 * kidx + 1], mods[3 * kidx + 2]
    u = normmod(h, gain[None, :] * (1.0 + scale), shift, BF16)
    act = matmul_swiglu(u, w13)
    return matmul_residual(act, w2, h, FFN_RES * gate)


def _seq(ctx_arr, lat_arr, rev):
    if rev:
        return jnp.concatenate([ctx_arr[::-1], lat_arr[::-1]], axis=0)
    return jnp.concatenate([ctx_arr, lat_arr], axis=0)


def _unseq(y, n_ctx, rev):
    yc, yl = y[:n_ctx], y[n_ctx:]
    if rev:
        return yc[::-1], yl[::-1]
    return yc, yl


def _rwkv_prep(u, p_rkv, P):
    rkv = _centred_mix(p_rkv, P['mu_rkv'])
    r, k, v = rkv[:, :D_A], rkv[:, D_A:2 * D_A], rkv[:, 2 * D_A:]
    xw = _centred_mix(u, P['mu_lora'][0]).astype(BF16)
    xa = _centred_mix(u, P['mu_lora'][1]).astype(BF16)
    xg = _centred_mix(u, P['mu_lora'][2]).astype(BF16)
    tw = jnp.tanh(matmul(xw, P['w1cat'])).astype(BF16)
    ta = matmul(xa, P['a1cat']).astype(BF16)
    tg = jax.nn.sigmoid(matmul(xg, P['g1'])).astype(BF16)
    g = matmul(tg, P['g2'])
    kk = (k * P['k_k']).reshape(-1, H_A, HEAD_A)
    kk = kk / jnp.maximum(jnp.sqrt(jnp.sum(kk * kk, axis=-1, keepdims=True)), 1e-12)
    kk = kk.reshape(-1, D_A)
    lws, ks, as_ = [], [], []
    for d in range(2):
        rd = P['w1cat'].shape[1] // 2
        lw = matmul(tw[:, d * rd:(d + 1) * rd], P['w2'][d]) + P['w0'][d]
        w_raw = -jax.nn.softplus(-lw) - 0.5
        lws.append(-jnp.exp(w_raw))
        ra = P['a1cat'].shape[1] // 2
        za = matmul(ta[:, d * ra:(d + 1) * ra], P['a2'][d]) + P['a0'][d]
        a = jax.nn.sigmoid(za)
        as_.append(a)
        ks.append(k * (1.0 + (a - 1.0) * P['k_a']))
    return dict(r=r, k=k, v=v, kk=kk, g=g, lw=lws, kd=ks, a=as_)


def _rwkv_out(y, prep, P):
    M = y.shape[0]
    yh = y.reshape(M, H_A, HEAD_A)
    mu = jnp.mean(yh, axis=-1, keepdims=True)
    var = jnp.mean((yh - mu) ** 2, axis=-1, keepdims=True)
    yn = ((yh - mu) * lax.rsqrt(var + RWKV_LN_EPS)).reshape(M, D_A) * P['ln_w'] + P['ln_b']
    rh = prep['r'].reshape(M, H_A, HEAD_A)
    dots = sum(jnp.sum(rh * prep['kd'][d].reshape(M, H_A, HEAD_A) * P['r_k'], axis=-1, keepdims=True)
               for d in range(2))
    bonus = (dots * prep['v'].reshape(M, H_A, HEAD_A)).reshape(M, D_A)
    return (yn + bonus) * prep['g']


def _gla_prep(u_bf, p_gla, P):
    q = p_gla[:, :D_B_K] * (DK_B ** -0.5)
    k = p_gla[:, D_B_K:2 * D_B_K]
    v = p_gla[:, 2 * D_B_K:2 * D_B_K + D_B_V]
    go = p_gla[:, 2 * D_B_K + D_B_V:]
    t = matmul(u_bf, P['ga1pad']).astype(BF16)
    las = []
    for d in range(2):
        z = matmul(t, P['ga2pad'][d]) + P['gab'][d]
        las.append(jax.nn.log_sigmoid(z) / GLA_GATE_NORM)
    return dict(q=q, k=k, v=v, go=go, la=las)


def _gla_out(o, go, gain):
    M = o.shape[0]
    oh = o.reshape(M, H_B, DV_B)
    y = oh * lax.rsqrt(jnp.mean(oh * oh, axis=-1, keepdims=True) + EPS) * gain
    return y.reshape(M, D_B_V) * _silu(go)


def _mixer_ab(u_lat, u_ctx, P):
    n_ctx = u_ctx.shape[0]
    outs = []
    preps = []
    for u in (u_ctx, u_lat):
        u_bf = u.astype(BF16)
        p = matmul(u_bf, P['w_in'])
        ra = _rwkv_prep(u, p[:, :3 * D_A], P)
        gb = _gla_prep(u_bf, p[:, 3 * D_A:], P)
        preps.append((ra, gb))
    (ra_c, gb_c), (ra_l, gb_l) = preps
    ya_c = ya_l = yb_c = yb_l = 0.0
    for d, rev in ((0, False), (1, True)):
        sq = lambda key, sub=None: _seq(ra_c[key] if sub is None else ra_c[key][sub],
                                        ra_l[key] if sub is None else ra_l[key][sub], rev)
        y = rwkv_scan(sq('r'), sq('lw', d), sq('kd', d), sq('v'), sq('kk'), sq('a', d))
        yc, yl = _unseq(y, n_ctx, rev)
        ya_c, ya_l = ya_c + yc, ya_l + yl
        sg = lambda key, sub=None: _seq(gb_c[key] if sub is None else gb_c[key][sub],
                                        gb_l[key] if sub is None else gb_l[key][sub], rev)
        o = gla_scan(sg('q'), sg('k'), sg('v'), sg('la', d))
        oc, ol = _unseq(o, n_ctx, rev)
        yb_c, yb_l = yb_c + oc, yb_l + ol
    for ya, yb, ra, gb in ((ya_c, yb_c, ra_c, gb_c), (ya_l, yb_l, ra_l, gb_l)):
        cat = jnp.concatenate([_rwkv_out(ya, ra, P), _gla_out(yb, gb['go'], P['g_norm'])], axis=-1)
        outs.append(cat.astype(BF16))
    return outs[1], outs[0]


def _head_rmsnorm(x, gain, nheads):
    M = x.shape[0]
    xh = x.reshape(M, nheads, HEAD_C)
    return xh * lax.rsqrt(jnp.mean(xh * xh, axis=-1, keepdims=True) + EPS) * gain


def _rope(xh, cos, sin):
    M, H, _ = xh.shape
    xr = xh.reshape(M, H, 2, 2, ROPE_PAIRS)
    x1, x2 = xr[..., 0, :], xr[..., 1, :]
    c, s = cos[:, None], sin[:, None]
    return jnp.stack([x1 * c - x2 * s, x2 * c + x1 * s], axis=-2).reshape(M, H, HEAD_C)


def _mixer_c(u_lat, u_ctx, P, need_ctx):
    T = u_lat.shape[0]
    rows = T // GRID_W
    rr = jnp.repeat(jnp.arange(rows), GRID_W)
    cc = jnp.tile(jnp.arange(GRID_W), rows)
    inv_freq = ROPE_THETA ** (-jnp.arange(ROPE_PAIRS, dtype=F32) / ROPE_PAIRS)
    ang = jnp.stack([rr, cc], axis=-1).astype(F32)[:, :, None] * inv_freq
    cos, sin = jnp.cos(ang), jnp.sin(ang)

    def qkv(u, rope):
        p = matmul(u, P['w_in'])
        q = _head_rmsnorm(p[:, :D_MODEL], P['q_gain'], H_C)
        k = _head_rmsnorm(p[:, D_MODEL:D_MODEL + KV_DIM], P['k_gain'], KV_C)
        v = p[:, D_MODEL + KV_DIM:]
        if rope:
            q, k = _rope(q, cos, sin), _rope(k, cos, sin)
        q = (q * HEAD_C ** -0.5).reshape(-1, D_MODEL).astype(BF16)
        return q, k.reshape(-1, KV_DIM).astype(BF16), v.astype(BF16)

    q_l, k_l, v_l = qkv(u_lat, True)
    q_c, k_c, v_c = qkv(u_ctx, False)
    o_l = flash_gqa(q_l, jnp.concatenate([k_c, k_l], axis=0), jnp.concatenate([v_c, v_l], axis=0))
    o_c = flash_gqa(q_c, k_c, v_c) if need_ctx else None
    return o_l, o_c


def kernel(x, c, ctx, c_ctx, ada_down, ada_up, ada_bias, norm_gains, final_gain, ffn_w13, ffn_w2, ab_w_in, ab_w_out, rwkv_mu_rkv, rwkv_mu_lora, rwkv_w1, rwkv_w2, rwkv_w0, rwkv_a1, rwkv_a2, rwkv_a0, rwkv_g1, rwkv_g2, rwkv_k_k, rwkv_k_a, rwkv_r_k, rwkv_ln_w, rwkv_ln_b, gla_a1, gla_a2, gla_ab, gla_norm, attn_w_in, attn_w_out, attn_q_norm, attn_k_norm):
    h_lat, h_ctx = x[0], ctx[0]
    D = D_MODEL
    bf = lambda t: t.astype(BF16)
    cvec = jnp.zeros((16, D), F32).at[0].set(c[0]).at[1].set(c_ctx)
    cs = bf(_silu(cvec))
    for layer in range(DEPTH):
        last = layer == DEPTH - 1
        i = layer // 2
        m = matmul(bf(matmul(cs, bf(ada_down[layer]))), bf(ada_up[layer])) + ada_bias[layer]
        m_l = [m[0:1, j * D:(j + 1) * D] for j in range(N_MOD)]
        m_c = [m[1:2, j * D:(j + 1) * D] for j in range(N_MOD)]
        gains = norm_gains[layer]
        w13a, w2a = bf(ffn_w13[layer, 0]), bf(ffn_w2[layer, 0])
        h_lat = _ffn_half(h_lat, m_l, 0, gains[0], w13a, w2a)
        h_ctx = _ffn_half(h_ctx, m_c, 0, gains[0], w13a, w2a)
        if layer % 2 == 0:
            u_lat = normmod(h_lat, gains[1][None] * (1.0 + m_l[4]), m_l[3], F32)
            u_ctx = normmod(h_ctx, gains[1][None] * (1.0 + m_c[4]), m_c[3], F32)
            ga1 = jnp.concatenate([gla_a1[i, 0], gla_a1[i, 1]], axis=1)
            rg = gla_a1.shape[-1]
            ga1pad = jnp.pad(ga1, ((0, 0), (0, LANES - 2 * rg)))
            ga2pad = [jnp.pad(gla_a2[i, d], ((d * rg, LANES - (d + 1) * rg), (0, 0))) for d in range(2)]
            P = dict(
                w_in=bf(ab_w_in[i]), mu_rkv=rwkv_mu_rkv[i], mu_lora=rwkv_mu_lora[i],
                w1cat=bf(jnp.concatenate([rwkv_w1[i, 0], rwkv_w1[i, 1]], axis=1)),
                w2=[bf(rwkv_w2[i, d]) for d in range(2)], w0=rwkv_w0[i],
                a1cat=bf(jnp.concatenate([rwkv_a1[i, 0], rwkv_a1[i, 1]], axis=1)),
                a2=[bf(rwkv_a2[i, d]) for d in range(2)], a0=rwkv_a0[i],
                g1=bf(rwkv_g1[i]), g2=bf(rwkv_g2[i]), k_k=rwkv_k_k[i], k_a=rwkv_k_a[i],
                r_k=rwkv_r_k[i], ln_w=rwkv_ln_w[i], ln_b=rwkv_ln_b[i],
                ga1pad=bf(ga1pad), ga2pad=[bf(t) for t in ga2pad], gab=gla_ab[i], g_norm=gla_norm[i])
            y_lat, y_ctx = _mixer_ab(u_lat, u_ctx, P)
            w_out = bf(ab_w_out[i])
        else:
            u_lat = normmod(h_lat, gains[1][None] * (1.0 + m_l[4]), m_l[3], BF16)
            u_ctx = normmod(h_ctx, gains[1][None] * (1.0 + m_c[4]), m_c[3], BF16)
            P = dict(w_in=bf(attn_w_in[i]), q_gain=attn_q_norm[i], k_gain=attn_k_norm[i])
            y_lat, y_ctx = _mixer_c(u_lat, u_ctx, P, not last)
            w_out = bf(attn_w_out[i])
        h_lat = matmul_residual(y_lat, w_out, h_lat, m_l[5])
        w13b, w2b = bf(ffn_w13[layer, 1]), bf(ffn_w2[layer, 1])
        h_lat = _ffn_half(h_lat, m_l, 2, gains[2], w13b, w2b)
        if not last:
            h_ctx = matmul_residual(y_ctx, w_out, h_ctx, m_c[5])
            h_ctx = _ffn_half(h_ctx, m_c, 2, gains[2], w13b, w2b)
    out = normmod(h_lat, final_gain[None, :], jnp.zeros((1, D), F32), F32)
    return out[None]
```

```python
import functools

import jax
import jax.numpy as jnp
from jax import lax
from jax.experimental import pallas as pl
from jax.experimental.pallas import tpu as pltpu

F32 = jnp.float32
BF16 = jnp.bfloat16

D_MODEL = 4096
DEPTH = 4
GRID_W = 64
N_MOD = 9
D_FF = 6144
FFN_RES = 0.5
EPS = 1e-6
D_A = D_MODEL // 2
HEAD_A = 64
H_A = D_A // HEAD_A
RWKV_LN_EPS = 64e-5
D_B_K = D_MODEL // 4
D_B_V = D_MODEL // 2
H_B = 4
DK_B = D_B_K // H_B
DV_B = D_B_V // H_B
GLA_GATE_NORM = 16.0
HEAD_C = 128
H_C = D_MODEL // HEAD_C
KV_C = 8
G_C = H_C // KV_C
KV_DIM = KV_C * HEAD_C
ROPE_PAIRS = HEAD_C // 4
ROPE_THETA = 10000.0
LOG2E = 1.4426950408889634

LANES = 128
SCAN_CHUNK = 64
VMEM_LIMIT = 56 * 1024 * 1024


def _pick(n, prefs):
    for p in prefs:
        if n % p == 0:
            return p
    return n


def _cparams(sem):
    return pltpu.CompilerParams(dimension_semantics=sem, vmem_limit_bytes=VMEM_LIMIT)


def _nt(a, b):
    return lax.dot_general(a, b, (((1,), (1,)), ((), ())), preferred_element_type=F32)


def _tn(a, b):
    return lax.dot_general(a, b, (((0,), (0,)), ((), ())), preferred_element_type=F32)


def _mm(a, b):
    return jnp.dot(a, b, preferred_element_type=F32)


def _mm_kernel(x_ref, w_ref, o_ref):
    o_ref[...] = jnp.dot(x_ref[...], w_ref[...], preferred_element_type=F32).astype(o_ref.dtype)


def _mm_swiglu_kernel(x_ref, wg_ref, wu_ref, o_ref):
    x = x_ref[...]
    g = jnp.dot(x, wg_ref[...], preferred_element_type=F32)
    u = jnp.dot(x, wu_ref[...], preferred_element_type=F32)
    o_ref[...] = (g / (1.0 + jnp.exp(-g)) * u).astype(o_ref.dtype)


def _mm_res_kernel(x_ref, w_ref, h_ref, c_ref, o_ref):
    y = jnp.dot(x_ref[...], w_ref[...], preferred_element_type=F32)
    o_ref[...] = h_ref[...] + c_ref[...] * y


def _mm_tiles(M, K, N):
    tm = _pick(M, (1024, 512, 256, 128, 64, 32, 16))
    tn = _pick(N, (1024, 512, 256, 128))
    if K > 4096:
        tn = min(tn, 512)
    return tm, tn


def matmul(x, w, out_dtype=F32):
    M, K = x.shape
    N = w.shape[1]
    tm, tn = _mm_tiles(M, K, N)
    return pl.pallas_call(
        _mm_kernel,
        grid=(M // tm, N // tn),
        in_specs=[pl.BlockSpec((tm, K), lambda i, j: (i, 0)),
                  pl.BlockSpec((K, tn), lambda i, j: (0, j))],
        out_specs=pl.BlockSpec((tm, tn), lambda i, j: (i, j)),
        out_shape=jax.ShapeDtypeStruct((M, N), out_dtype),
        compiler_params=_cparams(("parallel", "parallel")),
    )(x, w)


def matmul_swiglu(x, w13):
    M, K = x.shape
    F = w13.shape[1] // 2
    tm, _ = _mm_tiles(M, K, F)
    tn = _pick(F, (512, 256, 128))
    nf = F // tn
    return pl.pallas_call(
        _mm_swiglu_kernel,
        grid=(M // tm, nf),
        in_specs=[pl.BlockSpec((tm, K), lambda i, j: (i, 0)),
                  pl.BlockSpec((K, tn), lambda i, j: (0, j)),
                  pl.BlockSpec((K, tn), lambda i, j: (0, j + nf))],
        out_specs=pl.BlockSpec((tm, tn), lambda i, j: (i, j)),
        out_shape=jax.ShapeDtypeStruct((M, F), BF16),
        compiler_params=_cparams(("parallel", "parallel")),
    )(x, w13, w13)


def matmul_residual(x, w, h, coef):
    M, K = x.shape
    N = w.shape[1]
    tm, tn = _mm_tiles(M, K, N)
    tn = min(tn, 512)
    return pl.pallas_call(
        _mm_res_kernel,
        grid=(M // tm, N // tn),
        in_specs=[pl.BlockSpec((tm, K), lambda i, j: (i, 0)),
                  pl.BlockSpec((K, tn), lambda i, j: (0, j)),
                  pl.BlockSpec((tm, tn), lambda i, j: (i, j)),
                  pl.BlockSpec((1, tn), lambda i, j: (0, j))],
        out_specs=pl.BlockSpec((tm, tn), lambda i, j: (i, j)),
        out_shape=jax.ShapeDtypeStruct((M, N), F32),
        compiler_params=_cparams(("parallel", "parallel")),
    )(x, w, h, coef)


def _normmod_kernel(h_ref, g_ref, s_ref, o_ref):
    x = h_ref[...]
    y = x * lax.rsqrt(jnp.mean(x * x, axis=-1, keepdims=True) + EPS)
    o_ref[...] = (y * g_ref[...] + s_ref[...]).astype(o_ref.dtype)


def normmod(h, g, s, out_dtype):
    M, D = h.shape
    tm = _pick(M, (256, 128, 64, 32, 16, 8))
    return pl.pallas_call(
        _normmod_kernel,
        grid=(M // tm,),
        in_specs=[pl.BlockSpec((tm, D), lambda i: (i, 0)),
                  pl.BlockSpec((1, D), lambda i: (0, 0)),
                  pl.BlockSpec((1, D), lambda i: (0, 0))],
        out_specs=pl.BlockSpec((tm, D), lambda i: (i, 0)),
        out_shape=jax.ShapeDtypeStruct((M, D), out_dtype),
        compiler_params=_cparams(("parallel",)),
    )(h, g, s)


def _flash_kernel(q_ref, k_ref, v_ref, o_ref, m_sc, acc_sc):
    j = pl.program_id(2)
    reps = k_ref.shape[0] // LANES

    @pl.when(j == 0)
    def _():
        m_sc[...] = jnp.full(m_sc.shape, -1e30, F32)
        acc_sc[...] = jnp.zeros(acc_sc.shape, F32)

    k = k_ref[...]
    v = v_ref[...]
    heads = range(G_C)
    s = [_nt(q_ref[:, g * HEAD_C:(g + 1) * HEAD_C], k) for g in heads]
    m_prev = [m_sc[g] for g in heads]
    m_new = [jnp.maximum(m_prev[g], jnp.max(s[g], axis=-1, keepdims=True)) for g in heads]
    p = [jnp.exp2((s[g] - jnp.concatenate([m_new[g]] * reps, axis=1)).astype(BF16)) for g in heads]
    for g in heads:
        alpha = jnp.exp2(m_prev[g] - m_new[g])
        acc_sc[g] = jnp.concatenate([alpha, alpha], axis=1) * acc_sc[g] + _mm(p[g], v)
        m_sc[g] = m_new[g]

    @pl.when(j == pl.num_programs(2) - 1)
    def _():
        for g in heads:
            a = acc_sc[g]
            o_ref[:, g * HEAD_C:(g + 1) * HEAD_C] = (a[:, :HEAD_C] / a[:, HEAD_C:]).astype(o_ref.dtype)


def flash_gqa(q, k, v_ext):
    Lq = q.shape[0]
    Lk = k.shape[0]
    tq = _pick(Lq, (512, 256, 128))
    tk = _pick(Lk, (1280, 1024, 640, 512, 256, 128))
    gw = G_C * HEAD_C
    return pl.pallas_call(
        _flash_kernel,
        grid=(KV_C, Lq // tq, Lk // tk),
        in_specs=[pl.BlockSpec((tq, gw), lambda n, i, j: (i, n)),
                  pl.BlockSpec((tk, HEAD_C), lambda n, i, j: (j, n)),
                  pl.BlockSpec((tk, 2 * HEAD_C), lambda n, i, j: (j, n))],
        out_specs=pl.BlockSpec((tq, gw), lambda n, i, j: (i, n)),
        out_shape=jax.ShapeDtypeStruct((Lq, H_C * HEAD_C), BF16),
        scratch_shapes=[pltpu.VMEM((G_C, tq, LANES), F32),
                        pltpu.VMEM((G_C, tq, 2 * HEAD_C), F32)],
        compiler_params=_cparams(("parallel", "parallel", "arbitrary")),
    )(q, k, v_ext)


def _mm_split3(m, x):
    hi = x.astype(BF16)
    r1 = x - hi.astype(F32)
    mid = r1.astype(BF16)
    lo = (r1 - mid.astype(F32)).astype(BF16)
    return _mm(m, hi) + _mm(m, mid) + _mm(m, lo)


def _chunk_block(c, n_chunks, n_ctx_chunks, rev):
    if not rev:
        return c
    return jnp.where(c < n_ctx_chunks, n_ctx_chunks - 1 - c, n_chunks + n_ctx_chunks - 1 - c)


def _rwkv_kernel(r_ref, lw_ref, k_ref, v_ref, kk_ref, a_ref, y_ref, s_sc, *, npairs, rev):
    C = SCAN_CHUNK
    C2 = 2 * C

    @pl.when(pl.program_id(1) == 0)
    def _():
        s_sc[...] = jnp.zeros(s_sc.shape, F32)

    row = lax.broadcasted_iota(jnp.int32, (C, C), 0)
    col = lax.broadcasted_iota(jnp.int32, (C, C), 1)
    tri = ((col >= row) if rev else (col <= row)).astype(BF16)
    row2 = lax.broadcasted_iota(jnp.int32, (C2, C2), 0)
    col2 = lax.broadcasted_iota(jnp.int32, (C2, C2), 1)
    same = (row2 // C) == (col2 // C)
    strict2 = same & ((col2 > row2) if rev else (col2 < row2))
    incl2 = same & ((col2 >= row2) if rev else (col2 <= row2))
    eye2 = (row2 == col2).astype(F32)
    last = 0 if rev else C - 1
    head0 = lax.broadcasted_iota(jnp.int32, (C, LANES), 1) < HEAD_A

    lw = lw_ref[...]
    cl = _mm_split3(tri, lw)
    cl_last = cl[last:last + 1, :]
    kk = kk_ref[...]
    k = k_ref[...]
    b = a_ref[...] * kk
    p_inv = jnp.exp(-cl)
    tail = jnp.exp(cl_last - cl)
    kkd = kk * jnp.exp(cl - lw)
    rd = r_ref[...] * jnp.exp(cl)
    kinv = k * p_inv
    binv = b * p_inv
    kdec = k * tail
    nbdec = -(b * tail)
    v = v_ref[...]
    decay_last = jnp.exp(cl_last)

    def stack(x, p):
        xs = x[:, p * LANES:(p + 1) * LANES]
        return jnp.concatenate([jnp.where(head0, xs, 0.0), jnp.where(head0, 0.0, xs)], axis=0).astype(BF16)

    pairs = range(npairs)
    cat0 = lambda a, b_: jnp.concatenate([a, b_], axis=0)
    lhs = [cat0(stack(kkd, p), stack(rd, p)) for p in pairs]
    rhs_g = [cat0(stack(kinv, p), stack(binv, p)) for p in pairs]
    dec2 = [cat0(stack(kdec, p), stack(nbdec, p)) for p in pairs]
    v2 = [stack(v, p) for p in pairs]
    G = [_nt(lhs[p], rhs_g[p]) for p in pairs]
    a_kk = [jnp.where(strict2, G[p][:C2, :C2], 0.0).astype(BF16) for p in pairs]
    n1 = [jnp.where(strict2, -G[p][:C2, C2:], 0.0).astype(BF16) for p in pairs]
    b_rkb = [jnp.concatenate([jnp.where(incl2, G[p][C2:, :C2], 0.0),
                              jnp.where(incl2, -G[p][C2:, C2:], 0.0)], axis=1).astype(BF16) for p in pairs]

    powers = [n1]
    for _ in range(5):
        prev = powers[-1]
        powers.append([_mm(prev[p], prev[p]).astype(BF16) for p in pairs])
    tinv = [eye2 + n1[p].astype(F32) for p in pairs]
    for npow in powers[1:]:
        tinv = [tinv[p] + _mm(npow[p], tinv[p].astype(BF16)) for p in pairs]
    tinv = [t.astype(BF16) for t in tinv]

    S = [s_sc[p] for p in pairs]
    inter = [_nt(lhs[p], S[p].astype(BF16)) for p in pairs]
    rhs = [(inter[p][:C2] + _mm(a_kk[p], v2[p])).astype(BF16) for p in pairs]
    u2 = [_mm(tinv[p], rhs[p]).astype(BF16) for p in pairs]
    vu = [cat0(v2[p], u2[p]) for p in pairs]
    y2 = [inter[p][C2:] + _mm(b_rkb[p], vu[p]) for p in pairs]
    for p in pairs:
        sl = slice(p * LANES, (p + 1) * LANES)
        y_ref[:, sl] = y2[p][:C] + y2[p][C:]
        s_sc[p] = S[p] * decay_last[:, sl] + _tn(vu[p], dec2[p])


def rwkv_scan(r, lw, k, v, kk, a, n_ctx, rev):
    L = r.shape[0]
    C = SCAN_CHUNK
    gw = 1024
    npairs = gw // LANES
    nc = L // C
    spec = pl.BlockSpec((C, gw), lambda g, c: (_chunk_block(c, nc, n_ctx // C, rev), g))
    return pl.pallas_call(
        functools.partial(_rwkv_kernel, npairs=npairs, rev=rev),
        grid=(D_A // gw, nc),
        in_specs=[spec] * 6,
        out_specs=spec,
        out_shape=jax.ShapeDtypeStruct((L, D_A), F32),
        scratch_shapes=[pltpu.VMEM((npairs, LANES, LANES), F32)],
        compiler_params=_cparams(("parallel", "arbitrary")),
    )(r, lw, k, v, kk, a)


def _gla_kernel(q_ref, k_ref, v_ref, g_ref, o_ref, s_sc, *, rev):
    C = SCAN_CHUNK

    @pl.when(pl.program_id(0) == 0)
    def _():
        s_sc[...] = jnp.zeros(s_sc.shape, F32)

    row = lax.broadcasted_iota(jnp.int32, (C, C), 0)
    col = lax.broadcasted_iota(jnp.int32, (C, C), 1)
    le = (col >= row) if rev else (col <= row)
    trow = lax.broadcasted_iota(jnp.int32, (C, 1), 0)
    mats = [jnp.where(le, 1.0, 0.0)]
    levels = []
    n = C
    while n >= 2:
        half = n // 2
        same = (row // n) == (col // n)
        row_2nd = ((row % n) >= half) != rev
        col_2nd = ((col % n) >= half) != rev
        mats.append(jnp.where(same & row_2nd & col_2nd & le, 1.0, 0.0)
                    - jnp.where(same & (~row_2nd) & (~col_2nd) & (~le), 1.0, 0.0))
        levels.append((same, ((trow % n) >= half) != rev))
        n = half
    mall = jnp.concatenate(mats, axis=0).astype(BF16)

    call = _mm_split3(mall, g_ref[...])
    b = call[:C]
    last = 0 if rev else C - 1
    b_last = b[last:last + 1, :]
    q = q_ref[...]
    k = k_ref[...]
    qe = (q * jnp.exp(b)).astype(BF16)
    kd = (k * jnp.exp(b_last - b)).astype(BF16)
    dec = jnp.exp(b_last)
    qk = q * k
    qts, kts = [], []
    for l, (_, second) in enumerate(levels):
        e = jnp.exp(-jnp.abs(call[(l + 1) * C:(l + 2) * C]))
        qts.append(jnp.where(second, q * e, 0.0).astype(BF16))
        kts.append(jnp.where(second, 0.0, k * e).astype(BF16))
    vb = v_ref[...].astype(BF16)

    heads = range(H_B)
    ks = [slice(h * DK_B, (h + 1) * DK_B) for h in heads]
    vs = [slice(h * DV_B, (h + 1) * DV_B) for h in heads]
    St = [s_sc[h] for h in heads]
    o_inter = [_nt(qe[:, ks[h]], St[h].astype(BF16)) for h in heads]
    att = [jnp.where(row == col, jnp.sum(qk[:, ks[h]], axis=-1, keepdims=True), 0.0) for h in heads]
    for l, (same, _) in enumerate(levels):
        att = [att[h] + jnp.where(same, _nt(qts[l][:, ks[h]], kts[l][:, ks[h]]), 0.0) for h in heads]
    for h in heads:
        o_ref[:, vs[h]] = o_inter[h] + _mm(att[h].astype(BF16), vb[:, vs[h]])
        s_sc[h] = St[h] * dec[:, ks[h]] + _tn(vb[:, vs[h]], kd[:, ks[h]])


def gla_scan(q, k, v, g, n_ctx, rev):
    L = q.shape[0]
    C = SCAN_CHUNK
    nc = L // C
    imap = lambda c: (_chunk_block(c, nc, n_ctx // C, rev), 0)
    kspec = pl.BlockSpec((C, D_B_K), imap)
    vspec = pl.BlockSpec((C, D_B_V), imap)
    return pl.pallas_call(
        functools.partial(_gla_kernel, rev=rev),
        grid=(nc,),
        in_specs=[kspec, kspec, vspec, kspec],
        out_specs=vspec,
        out_shape=jax.ShapeDtypeStruct((L, D_B_V), F32),
        scratch_shapes=[pltpu.VMEM((H_B, DV_B, DK_B), F32)],
        compiler_params=_cparams(("arbitrary",)),
    )(q, k, v, g)


def _silu(x):
    return x * jax.nn.sigmoid(x)


def _centred_mix(x, mu):
    prev = jnp.pad(x, ((1, 0), (0, 0)))[:-1]
    nxt = jnp.pad(x, ((0, 1), (0, 0)))[1:]
    return x + mu[0] * (prev - x) + mu[1] * (nxt - x)


def _ffn_half(h, mods, kidx, gain, w13, w2):
    shift, scale, gate = mods[3 * kidx], mods[3 * kidx + 1], mods[3 * kidx + 2]
    u = normmod(h, gain[None, :] * (1.0 + scale), shift, BF16)
    act = matmul_swiglu(u, w13)
    return matmul_residual(act, w2, h, FFN_RES * gate)


def _rwkv_prep(u, p_rkv, P):
    rkv = _centred_mix(p_rkv, P['mu_rkv'])
    r, k, v = rkv[:, :D_A], rkv[:, D_A:2 * D_A], rkv[:, 2 * D_A:]
    xw = _centred_mix(u, P['mu_lora'][0]).astype(BF16)
    xa = _centred_mix(u, P['mu_lora'][1]).astype(BF16)
    xg = _centred_mix(u, P['mu_lora'][2]).astype(BF16)
    tw = jnp.tanh(matmul(xw, P['w1cat'])).astype(BF16)
    ta = matmul(xa, P['a1cat']).astype(BF16)
    tg = jax.nn.sigmoid(matmul(xg, P['g1'])).astype(BF16)
    g = matmul(tg, P['g2'])
    kk = (k * P['k_k']).reshape(-1, H_A, HEAD_A)
    kk = kk / jnp.maximum(jnp.sqrt(jnp.sum(kk * kk, axis=-1, keepdims=True)), 1e-12)
    kk = kk.reshape(-1, D_A)
    lws, ks, as_ = [], [], []
    for d in range(2):
        rd = P['w1cat'].shape[1] // 2
        lw = matmul(tw[:, d * rd:(d + 1) * rd], P['w2'][d]) + P['w0'][d]
        w_raw = -jax.nn.softplus(-lw) - 0.5
        lws.append(-jnp.exp(w_raw))
        ra = P['a1cat'].shape[1] // 2
        za = matmul(ta[:, d * ra:(d + 1) * ra], P['a2'][d]) + P['a0'][d]
        a = jax.nn.sigmoid(za)
        as_.append(a)
        ks.append(k * (1.0 + (a - 1.0) * P['k_a']))
    return dict(r=r, k=k, v=v, kk=kk, g=g, lw=lws, kd=ks, a=as_)


def _rwkv_out(y, prep, P):
    M = y.shape[0]
    yh = y.reshape(M, H_A, HEAD_A)
    mu = jnp.mean(yh, axis=-1, keepdims=True)
    var = jnp.mean((yh - mu) ** 2, axis=-1, keepdims=True)
    yn = ((yh - mu) * lax.rsqrt(var + RWKV_LN_EPS)).reshape(M, D_A) * P['ln_w'] + P['ln_b']
    rh = prep['r'].reshape(M, H_A, HEAD_A)
    dots = sum(jnp.sum(rh * prep['kd'][d].reshape(M, H_A, HEAD_A) * P['r_k'], axis=-1, keepdims=True)
               for d in range(2))
    bonus = (dots * prep['v'].reshape(M, H_A, HEAD_A)).reshape(M, D_A)
    return (yn + bonus) * prep['g']


def _gla_prep(u_bf, p_gla, P):
    q = p_gla[:, :D_B_K] * (DK_B ** -0.5)
    k = p_gla[:, D_B_K:2 * D_B_K]
    v = p_gla[:, 2 * D_B_K:2 * D_B_K + D_B_V]
    go = p_gla[:, 2 * D_B_K + D_B_V:]
    t = matmul(u_bf, P['ga1pad']).astype(BF16)
    las = []
    for d in range(2):
        z = matmul(t, P['ga2pad'][d]) + P['gab'][d]
        las.append(jax.nn.log_sigmoid(z) / GLA_GATE_NORM)
    return dict(q=q, k=k, v=v, go=go, la=las)


def _gla_out(o, go, gain):
    M = o.shape[0]
    oh = o.reshape(M, H_B, DV_B)
    y = oh * lax.rsqrt(jnp.mean(oh * oh, axis=-1, keepdims=True) + EPS) * gain
    return y.reshape(M, D_B_V) * _silu(go)


def _mixer_ab(u_lat, u_ctx, P):
    n_ctx = u_ctx.shape[0]
    outs = []
    preps = []
    for u in (u_ctx, u_lat):
        u_bf = u.astype(BF16)
        p = matmul(u_bf, P['w_in'])
        ra = _rwkv_prep(u, p[:, :3 * D_A], P)
        gb = _gla_prep(u_bf, p[:, 3 * D_A:], P)
        preps.append((ra, gb))
    (ra_c, gb_c), (ra_l, gb_l) = preps
    cat = lambda dc, dl, key, sub=None: jnp.concatenate(
        [dc[key] if sub is None else dc[key][sub], dl[key] if sub is None else dl[key][sub]], axis=0)
    sq = functools.partial(cat, ra_c, ra_l)
    sg = functools.partial(cat, gb_c, gb_l)
    r, v, kk = sq('r'), sq('v'), sq('kk')
    gq, gk, gv = sg('q'), sg('k'), sg('v')
    ya = yb = 0.0
    for d, rev in ((0, False), (1, True)):
        ya = ya + rwkv_scan(r, sq('lw', d), sq('kd', d), v, kk, sq('a', d), n_ctx, rev)
        yb = yb + gla_scan(gq, gk, gv, sg('la', d), n_ctx, rev)
    ya_c, ya_l, yb_c, yb_l = ya[:n_ctx], ya[n_ctx:], yb[:n_ctx], yb[n_ctx:]
    for ya, yb, ra, gb in ((ya_c, yb_c, ra_c, gb_c), (ya_l, yb_l, ra_l, gb_l)):
        cat = jnp.concatenate([_rwkv_out(ya, ra, P), _gla_out(yb, gb['go'], P['g_norm'])], axis=-1)
        outs.append(cat.astype(BF16))
    return outs[1], outs[0]


def _head_rmsnorm(x, gain, nheads):
    M = x.shape[0]
    xh = x.reshape(M, nheads, HEAD_C)
    return xh * lax.rsqrt(jnp.mean(xh * xh, axis=-1, keepdims=True) + EPS) * gain


def _rope(xh, cos, sin):
    M, H, _ = xh.shape
    xr = xh.reshape(M, H, 2, 2, ROPE_PAIRS)
    x1, x2 = xr[..., 0, :], xr[..., 1, :]
    c, s = cos[:, None], sin[:, None]
    return jnp.stack([x1 * c - x2 * s, x2 * c + x1 * s], axis=-2).reshape(M, H, HEAD_C)


def _mixer_c(u_lat, u_ctx, P, need_ctx):
    T = u_lat.shape[0]
    rows = T // GRID_W
    rr = jnp.repeat(jnp.arange(rows), GRID_W)
    cc = jnp.tile(jnp.arange(GRID_W), rows)
    inv_freq = ROPE_THETA ** (-jnp.arange(ROPE_PAIRS, dtype=F32) / ROPE_PAIRS)
    ang = jnp.stack([rr, cc], axis=-1).astype(F32)[:, :, None] * inv_freq
    cos, sin = jnp.cos(ang), jnp.sin(ang)

    def qkv(u, rope):
        p = matmul(u, P['w_in'])
        q = _head_rmsnorm(p[:, :D_MODEL], P['q_gain'], H_C)
        k = _head_rmsnorm(p[:, D_MODEL:D_MODEL + KV_DIM], P['k_gain'], KV_C)
        v = p[:, D_MODEL + KV_DIM:]
        if rope:
            q, k = _rope(q, cos, sin), _rope(k, cos, sin)
        q = (q * (LOG2E * HEAD_C ** -0.5)).reshape(-1, D_MODEL).astype(BF16)
        vh = v.astype(BF16).reshape(-1, KV_C, HEAD_C)
        v_ext = jnp.concatenate([vh, jnp.ones_like(vh)], axis=-1).reshape(-1, 2 * KV_DIM)
        return q, k.reshape(-1, KV_DIM).astype(BF16), v_ext

    q_l, k_l, v_l = qkv(u_lat, True)
    q_c, k_c, v_c = qkv(u_ctx, False)
    o_l = flash_gqa(q_l, jnp.concatenate([k_c, k_l], axis=0), jnp.concatenate([v_c, v_l], axis=0))
    o_c = flash_gqa(q_c, k_c, v_c) if need_ctx else None
    return o_l, o_c


def kernel(x, c, ctx, c_ctx, ada_down, ada_up, ada_bias, norm_gains, final_gain, ffn_w13, ffn_w2, ab_w_in, ab_w_out, rwkv_mu_rkv, rwkv_mu_lora, rwkv_w1, rwkv_w2, rwkv_w0, rwkv_a1, rwkv_a2, rwkv_a0, rwkv_g1, rwkv_g2, rwkv_k_k, rwkv_k_a, rwkv_r_k, rwkv_ln_w, rwkv_ln_b, gla_a1, gla_a2, gla_ab, gla_norm, attn_w_in, attn_w_out, attn_q_norm, attn_k_norm):
    h_lat, h_ctx = x[0], ctx[0]
    D = D_MODEL
    bf = lambda t: t.astype(BF16)
    cvec = jnp.zeros((16, D), F32).at[0].set(c[0]).at[1].set(c_ctx)
    cs = bf(_silu(cvec))
    for layer in range(DEPTH):
        last = layer == DEPTH - 1
        i = layer // 2
        m = matmul(bf(matmul(cs, bf(ada_down[layer]))), bf(ada_up[layer])) + ada_bias[layer]
        m_l = [m[0:1, j * D:(j + 1) * D] for j in range(N_MOD)]
        m_c = [m[1:2, j * D:(j + 1) * D] for j in range(N_MOD)]
        gains = norm_gains[layer]
        w13a, w2a = bf(ffn_w13[layer, 0]), bf(ffn_w2[layer, 0])
        h_lat = _ffn_half(h_lat, m_l, 0, gains[0], w13a, w2a)
        h_ctx = _ffn_half(h_ctx, m_c, 0, gains[0], w13a, w2a)
        if layer % 2 == 0:
            u_lat = normmod(h_lat, gains[1][None] * (1.0 + m_l[4]), m_l[3], F32)
            u_ctx = normmod(h_ctx, gains[1][None] * (1.0 + m_c[4]), m_c[3], F32)
            ga1 = jnp.concatenate([gla_a1[i, 0], gla_a1[i, 1]], axis=1)
            rg = gla_a1.shape[-1]
            ga1pad = jnp.pad(ga1, ((0, 0), (0, LANES - 2 * rg)))
            ga2pad = [jnp.pad(gla_a2[i, d], ((d * rg, LANES - (d + 1) * rg), (0, 0))) for d in range(2)]
            P = dict(
                w_in=bf(ab_w_in[i]), mu_rkv=rwkv_mu_rkv[i], mu_lora=rwkv_mu_lora[i],
                w1cat=bf(jnp.concatenate([rwkv_w1[i, 0], rwkv_w1[i, 1]], axis=1)),
                w2=[bf(rwkv_w2[i, d]) for d in range(2)], w0=rwkv_w0[i],
                a1cat=bf(jnp.concatenate([rwkv_a1[i, 0], rwkv_a1[i, 1]], axis=1)),
                a2=[bf(rwkv_a2[i, d]) for d in range(2)], a0=rwkv_a0[i],
                g1=bf(rwkv_g1[i]), g2=bf(rwkv_g2[i]), k_k=rwkv_k_k[i], k_a=rwkv_k_a[i],
                r_k=rwkv_r_k[i], ln_w=rwkv_ln_w[i], ln_b=rwkv_ln_b[i],
                ga1pad=bf(ga1pad), ga2pad=[bf(t) for t in ga2pad], gab=gla_ab[i], g_norm=gla_norm[i])
            y_lat, y_ctx = _mixer_ab(u_lat, u_ctx, P)
            w_out = bf(ab_w_out[i])
        else:
            u_lat = normmod(h_lat, gains[1][None] * (1.0 + m_l[4]), m_l[3], BF16)
            u_ctx = normmod(h_ctx, gains[1][None] * (1.0 + m_c[4]), m_c[3], BF16)
            P = dict(w_in=bf(attn_w_in[i]), q_gain=attn_q_norm[i], k_gain=attn_k_norm[i])
            y_lat, y_ctx = _mixer_c(u_lat, u_ctx, P, not last)
            w_out = bf(attn_w_out[i])
        h_lat = matmul_residual(y_lat, w_out, h_lat, m_l[5])
        w13b, w2b = bf(ffn_w13[layer, 1]), bf(ffn_w2[layer, 1])
        h_lat = _ffn_half(h_lat, m_l, 2, gains[2], w13b, w2b)
        if not last:
            h_ctx = matmul_residual(y_ctx, w_out, h_ctx, m_c[5])
            h_ctx = _ffn_half(h_ctx, m_c, 2, gains[2], w13b, w2b)
    out = normmod(h_lat, final_gain[None, :], jnp.zeros((1, D), F32), F32)
    return out[None]
```

```python
import functools
import math

import jax
import jax.numpy as jnp
from jax import lax
from jax.experimental import pallas as pl
from jax.experimental.pallas import tpu as pltpu

F32 = jnp.float32
BF16 = jnp.bfloat16

D_MODEL = 4096
DEPTH = 4
GRID_W = 64
N_MOD = 9
D_FF = 6144
FFN_RES = 0.5
EPS = 1e-6
D_A = D_MODEL // 2
HEAD_A = 64
H_A = D_A // HEAD_A
R_DECAY = 128
R_AAA = 128
R_GATE = 256
RWKV_LN_EPS = 64e-5
D_B_K = D_MODEL // 4
D_B_V = D_MODEL // 2
H_B = 4
DK_B = D_B_K // H_B
DV_B = D_B_V // H_B
R_GLA_GATE = 16
GLA_GATE_NORM = 16.0
D_AB_IN = 3 * D_A + 2 * D_B_K + 2 * D_B_V
HEAD_C = 128
H_C = D_MODEL // HEAD_C
KV_C = 8
G_C = H_C // KV_C
KV_DIM = KV_C * HEAD_C
ROPE_PAIRS = HEAD_C // 4
ROPE_THETA = 10000.0
LOG2E = 1.4426950408889634

LANES = 128
SUBLANES = 8
SCAN_CHUNK = 64
VMEM_LIMIT = 56 * 1024 * 1024

LORA_W = 3072
AB_COLS = D_AB_IN + LORA_W
OFF_ZW, OFF_ZA, OFF_ZG = 0, 3 * 2 * R_DECAY, 3 * 2 * R_DECAY + 3 * 2 * R_AAA
OFF_ZGA = OFF_ZG + 3 * R_GATE


def _pick(n, prefs):
    for p in prefs:
        if n % p == 0:
            return p
    return n


def _cparams(sem):
    return pltpu.CompilerParams(dimension_semantics=sem, vmem_limit_bytes=VMEM_LIMIT)


def _row_tile(n_lat, n_rows, prefs):
    return _pick(math.gcd(n_lat, n_rows - n_lat) if n_rows > n_lat else n_lat, prefs)


def _sigmoid(x):
    return 1.0 / (1.0 + jnp.exp(-x))


def _softplus(x):
    return jnp.maximum(x, 0.0) + jnp.log(1.0 + jnp.exp(-jnp.abs(x)))


def _nt(a, b):
    return lax.dot_general(a, b, (((1,), (1,)), ((), ())), preferred_element_type=F32)


def _tn(a, b):
    return lax.dot_general(a, b, (((0,), (0,)), ((), ())), preferred_element_type=F32)


def _mm(a, b):
    return jnp.dot(a, b, preferred_element_type=F32)


def _mm_kernel(x_ref, w_ref, o_ref):
    o_ref[...] = _mm(x_ref[...], w_ref[...]).astype(o_ref.dtype)


def _mm_swiglu_kernel(x_ref, wg_ref, wu_ref, o_ref):
    x = x_ref[...]
    g = _mm(x, wg_ref[...])
    u = _mm(x, wu_ref[...])
    o_ref[...] = (g * _sigmoid(g) * u).astype(o_ref.dtype)


def _mm_res_kernel(x_ref, w_ref, h_ref, c_ref, o_ref, *, n_lat):
    tm = x_ref.shape[0]
    rows = pl.program_id(0) * tm + lax.broadcasted_iota(jnp.int32, (tm, 1), 0)
    coef = jnp.where(rows >= n_lat, c_ref[1:2, :], c_ref[0:1, :])
    o_ref[...] = h_ref[...] + coef * _mm(x_ref[...], w_ref[...])


_TM_PREFS = (1280, 1024, 640, 512, 256, 128, 64, 32, 16)


def matmul(x, w, out_dtype=F32):
    M, K = x.shape
    N = w.shape[1]
    tm = _pick(M, _TM_PREFS)
    wide = tm * K * 2 <= 8 * 1024 * 1024
    tn = _pick(N, (1024, 512, 256, 128) if wide else (512, 256, 128))
    return pl.pallas_call(
        _mm_kernel,
        grid=(M // tm, N // tn),
        in_specs=[pl.BlockSpec((tm, K), lambda i, j: (i, 0)),
                  pl.BlockSpec((K, tn), lambda i, j: (0, j))],
        out_specs=pl.BlockSpec((tm, tn), lambda i, j: (i, j)),
        out_shape=jax.ShapeDtypeStruct((M, N), out_dtype),
        compiler_params=_cparams(("parallel", "parallel")),
    )(x, w)


def matmul_swiglu(x, w13):
    M, K = x.shape
    F = w13.shape[1] // 2
    tm = _pick(M, _TM_PREFS)
    tn = _pick(F, (512, 256, 128))
    nf = F // tn
    return pl.pallas_call(
        _mm_swiglu_kernel,
        grid=(M // tm, nf),
        in_specs=[pl.BlockSpec((tm, K), lambda i, j: (i, 0)),
                  pl.BlockSpec((K, tn), lambda i, j: (0, j)),
                  pl.BlockSpec((K, tn), lambda i, j: (0, j + nf))],
        out_specs=pl.BlockSpec((tm, tn), lambda i, j: (i, j)),
        out_shape=jax.ShapeDtypeStruct((M, F), BF16),
        compiler_params=_cparams(("parallel", "parallel")),
    )(x, w13, w13)


def matmul_residual(x, w, h, coef, n_lat):
    M, K = x.shape
    N = w.shape[1]
    tm = _pick(M, _TM_PREFS)
    tn = _pick(N, (256,) if K > D_MODEL else (512, 256, 128))
    return pl.pallas_call(
        functools.partial(_mm_res_kernel, n_lat=n_lat),
        grid=(M // tm, N // tn),
        in_specs=[pl.BlockSpec((tm, K), lambda i, j: (i, 0)),
                  pl.BlockSpec((K, tn), lambda i, j: (0, j)),
                  pl.BlockSpec((tm, tn), lambda i, j: (i, j)),
                  pl.BlockSpec((SUBLANES, tn), lambda i, j: (0, j))],
        out_specs=pl.BlockSpec((tm, tn), lambda i, j: (i, j)),
        out_shape=jax.ShapeDtypeStruct((M, N), F32),
        compiler_params=_cparams(("parallel", "parallel")),
    )(x, w, h, coef)


def _normmod_kernel(h_ref, g_ref, s_ref, o_ref):
    x = h_ref[...]
    y = x * lax.rsqrt(jnp.mean(x * x, axis=-1, keepdims=True) + EPS)
    o_ref[...] = (y * g_ref[...] + s_ref[...]).astype(o_ref.dtype)


def normmod(h, g, s, n_lat, n_rows, out_dtype):
    D = h.shape[1]
    tm = _row_tile(n_lat, n_rows, (256, 128, 64, 32, 16, 8))
    seg = lambda i: (jnp.where(i * tm >= n_lat, 1, 0), 0, 0)
    return pl.pallas_call(
        _normmod_kernel,
        grid=(n_rows // tm,),
        in_specs=[pl.BlockSpec((tm, D), lambda i: (i, 0)),
                  pl.BlockSpec((None, 1, D), seg),
                  pl.BlockSpec((None, 1, D), seg)],
        out_specs=pl.BlockSpec((tm, D), lambda i: (i, 0)),
        out_shape=jax.ShapeDtypeStruct((n_rows, D), out_dtype),
        compiler_params=_cparams(("parallel",)),
    )(h, g, s)


def _flash_kernel(q_ref, k_ref, v_ref, o_ref, m_sc, acc_sc):
    j = pl.program_id(2)
    reps = k_ref.shape[0] // LANES

    @pl.when(j == 0)
    def _():
        m_sc[...] = jnp.full(m_sc.shape, -1e30, F32)
        acc_sc[...] = jnp.zeros(acc_sc.shape, F32)

    k = k_ref[...]
    v = v_ref[...]
    heads = range(G_C)
    s = [_nt(q_ref[:, g * HEAD_C:(g + 1) * HEAD_C], k) for g in heads]
    m_prev = [m_sc[g] for g in heads]
    m_new = [jnp.maximum(m_prev[g], jnp.max(s[g], axis=-1, keepdims=True)) for g in heads]
    p = [jnp.exp2((s[g] - jnp.concatenate([m_new[g]] * reps, axis=1)).astype(BF16)) for g in heads]
    for g in heads:
        alpha = jnp.exp2(m_prev[g] - m_new[g])
        acc_sc[g] = jnp.concatenate([alpha, alpha], axis=1) * acc_sc[g] + _mm(p[g], v)
        m_sc[g] = m_new[g]

    @pl.when(j == pl.num_programs(2) - 1)
    def _():
        for g in heads:
            a = acc_sc[g]
            o_ref[:, g * HEAD_C:(g + 1) * HEAD_C] = (a[:, :HEAD_C] / a[:, HEAD_C:]).astype(o_ref.dtype)


def flash_gqa(q, k, v_ext, q_rows, k_rows):
    q0, Lq = q_rows
    k0, Lk = k_rows
    tq = _pick(math.gcd(q0, Lq), (512, 256, 128))
    tk = _pick(math.gcd(k0, Lk), (1280, 1024, 640, 512, 256, 128))
    qb, kb = q0 // tq, k0 // tk
    gw = G_C * HEAD_C
    return pl.pallas_call(
        _flash_kernel,
        grid=(KV_C, Lq // tq, Lk // tk),
        in_specs=[pl.BlockSpec((tq, gw), lambda n, i, j: (i + qb, n)),
                  pl.BlockSpec((tk, HEAD_C), lambda n, i, j: (j + kb, n)),
                  pl.BlockSpec((tk, 2 * HEAD_C), lambda n, i, j: (j + kb, n))],
        out_specs=pl.BlockSpec((tq, gw), lambda n, i, j: (i, n)),
        out_shape=jax.ShapeDtypeStruct((Lq, H_C * HEAD_C), BF16),
        scratch_shapes=[pltpu.VMEM((G_C, tq, LANES), F32),
                        pltpu.VMEM((G_C, tq, 2 * HEAD_C), F32)],
        compiler_params=_cparams(("parallel", "parallel", "arbitrary")),
    )(q, k, v_ext)


def _attn_prep_kernel(p_ref, cos_ref, sin_ref, qg_ref, kg_ref, q_ref, k_ref, v_ref):
    cos = cos_ref[...]
    sin = sin_ref[...]
    lane = lax.broadcasted_iota(jnp.int32, cos.shape, 1)
    first = (lane % (2 * ROPE_PAIRS)) < ROPE_PAIRS

    def norm_rope(x, gain):
        xn = x * lax.rsqrt(jnp.mean(x * x, axis=-1, keepdims=True) + EPS) * gain
        partner = jnp.where(first, pltpu.roll(xn, HEAD_C - ROPE_PAIRS, 1), pltpu.roll(xn, ROPE_PAIRS, 1))
        return xn * cos + partner * sin

    qg = qg_ref[...] * (LOG2E * HEAD_C ** -0.5)
    kg = kg_ref[...]
    for h in range(H_C):
        sl = slice(h * HEAD_C, (h + 1) * HEAD_C)
        q_ref[:, sl] = norm_rope(p_ref[:, sl], qg).astype(q_ref.dtype)
    ones = jnp.ones((p_ref.shape[0], HEAD_C), v_ref.dtype)
    for h in range(KV_C):
        sl = slice(h * HEAD_C, (h + 1) * HEAD_C)
        k_ref[:, sl] = norm_rope(p_ref[:, D_MODEL + h * HEAD_C:D_MODEL + (h + 1) * HEAD_C], kg).astype(k_ref.dtype)
        v_ref[:, 2 * h * HEAD_C:(2 * h + 1) * HEAD_C] = \
            p_ref[:, D_MODEL + KV_DIM + h * HEAD_C:D_MODEL + KV_DIM + (h + 1) * HEAD_C].astype(v_ref.dtype)
        v_ref[:, (2 * h + 1) * HEAD_C:(2 * h + 2) * HEAD_C] = ones


def attn_prep(p, cos, sin, q_gain, k_gain):
    L = p.shape[0]
    tm = _pick(L, (256, 128, 64, 32, 16, 8))
    row = lambda w: pl.BlockSpec((tm, w), lambda i: (i, 0))
    par = pl.BlockSpec((1, HEAD_C), lambda i: (0, 0))
    return pl.pallas_call(
        _attn_prep_kernel,
        grid=(L // tm,),
        in_specs=[row(D_MODEL + 2 * KV_DIM), row(HEAD_C), row(HEAD_C), par, par],
        out_specs=[row(D_MODEL), row(KV_DIM), row(2 * KV_DIM)],
        out_shape=[jax.ShapeDtypeStruct((L, D_MODEL), BF16),
                   jax.ShapeDtypeStruct((L, KV_DIM), BF16),
                   jax.ShapeDtypeStruct((L, 2 * KV_DIM), BF16)],
        compiler_params=_cparams(("parallel",)),
    )(p, cos, sin, q_gain, k_gain)


def _head_sum64(x):
    r_ = lax.broadcasted_iota(jnp.int32, (LANES, LANES), 0) // HEAD_A
    c_ = lax.broadcasted_iota(jnp.int32, (LANES, LANES), 1) // HEAD_A
    ones_bd = (r_ == c_).astype(BF16)
    hi = x.astype(BF16)
    lo = (x - hi.astype(F32)).astype(BF16)
    tiles = [_mm(hi[:, j:j + LANES], ones_bd) + _mm(lo[:, j:j + LANES], ones_bd) for j in range(0, x.shape[1], LANES)]
    return jnp.concatenate(tiles, axis=1)


def _ab_prep_kernel(x_ref, xp_ref, xn_ref, z_ref, zp_ref, zn_ref, mu_ref, w2_ref, w0_ref, a2_ref, a0_ref,
                    g2_ref, kk_ref, ga2_ref, gab_ref,
                    r_o, k_o, v_o, kkn_o, lw0_o, lw1_o, a0_o, a1_o, g_o, la0_o, la1_o, *, n_lat, n_rows):
    tm = x_ref.shape[0]
    i = pl.program_id(0)
    loc = lax.broadcasted_iota(jnp.int32, (tm, 1), 0)
    rows = i * tm + loc
    seg_first = (rows == 0) | (rows == n_lat)
    seg_last = (rows == n_lat - 1) | (rows == n_rows - 1)

    def prev(x, halo):
        y = jnp.where(loc == 0, halo[SUBLANES - 1:SUBLANES, :], pltpu.roll(x, 1, 0))
        return jnp.where(seg_first, 0.0, y)

    def nxt(x, halo):
        y = jnp.where(loc == tm - 1, halo[0:1, :], pltpu.roll(x, tm - 1, 0))
        return jnp.where(seg_last, 0.0, y)

    x = x_ref[...]
    mu = mu_ref[...]
    rkv = x + mu[0:1, :] * (prev(x, xp_ref[...]) - x) + mu[1:2, :] * (nxt(x, xn_ref[...]) - x)
    r_o[...] = rkv[:, :D_A]
    k = rkv[:, D_A:2 * D_A]
    k_o[...] = k
    v_o[...] = rkv[:, 2 * D_A:]
    kk = k * kk_ref[...]
    kkn_o[...] = kk / jnp.maximum(jnp.sqrt(_head_sum64(kk * kk)), 1e-12)

    def lora(off, width):
        c = z_ref[:, off:off + width]
        p_ = prev(z_ref[:, off + width:off + 2 * width], zp_ref[:, off + width:off + 2 * width])
        n_ = nxt(z_ref[:, off + 2 * width:off + 3 * width], zn_ref[:, off + 2 * width:off + 3 * width])
        return c + p_ + n_

    zw = jnp.tanh(lora(OFF_ZW, 2 * R_DECAY)).astype(BF16)
    za = lora(OFF_ZA, 2 * R_AAA).astype(BF16)
    zg = _sigmoid(lora(OFF_ZG, R_GATE)).astype(BF16)
    zga = z_ref[:, OFF_ZGA:OFF_ZGA + LANES].astype(BF16)
    g_o[...] = _mm(zg, g2_ref[...])
    for d, (lw_o, a_o, la_o) in enumerate(((lw0_o, a0_o, la0_o), (lw1_o, a1_o, la1_o))):
        lw = _mm(zw[:, d * R_DECAY:(d + 1) * R_DECAY], w2_ref[d]) + w0_ref[d:d + 1, :]
        lw_o[...] = -jnp.exp(-_softplus(-lw) - 0.5)
        a_o[...] = _sigmoid(_mm(za[:, d * R_AAA:(d + 1) * R_AAA], a2_ref[d]) + a0_ref[d:d + 1, :])
        zz = _mm(zga, ga2_ref[d]) + gab_ref[d:d + 1, :]
        la_o[...] = -_softplus(-zz) * (1.0 / GLA_GATE_NORM)


def ab_prep(p_all, P, n_lat):
    L = p_all.shape[0]
    tm = _row_tile(n_lat, L, (128, 64, 32, 16, 8))
    nb8 = tm // SUBLANES
    last8 = L // SUBLANES - 1
    wx, cz = 3 * D_A, D_AB_IN // LORA_W
    full = lambda shape: pl.BlockSpec(shape, lambda i: (0,) * len(shape))
    outw = lambda w: pl.BlockSpec((tm, w), lambda i: (i, 0))
    in_specs = [
        pl.BlockSpec((tm, wx), lambda i: (i, 0)),
        pl.BlockSpec((SUBLANES, wx), lambda i: (jnp.maximum(i * nb8 - 1, 0), 0)),
        pl.BlockSpec((SUBLANES, wx), lambda i: (jnp.minimum((i + 1) * nb8, last8), 0)),
        pl.BlockSpec((tm, LORA_W), lambda i: (i, cz)),
        pl.BlockSpec((SUBLANES, LORA_W), lambda i: (jnp.maximum(i * nb8 - 1, 0), cz)),
        pl.BlockSpec((SUBLANES, LORA_W), lambda i: (jnp.minimum((i + 1) * nb8, last8), cz)),
        full((2, wx)), full((2, R_DECAY, D_A)), full((2, D_A)), full((2, R_AAA, D_A)), full((2, D_A)),
        full((R_GATE, D_A)), full((1, D_A)), full((2, LANES, D_B_K)), full((2, D_B_K)),
    ]
    outs = [D_A] * 9 + [D_B_K] * 2
    return pl.pallas_call(
        functools.partial(_ab_prep_kernel, n_lat=n_lat, n_rows=L),
        grid=(L // tm,),
        in_specs=in_specs,
        out_specs=[outw(w) for w in outs],
        out_shape=[jax.ShapeDtypeStruct((L, w), F32) for w in outs],
        compiler_params=_cparams(("parallel",)),
    )(p_all, p_all, p_all, p_all, p_all, p_all, P['mu_rkv'], P['w2'], P['w0'], P['a2'], P['a0'],
      P['g2'], P['k_k'], P['ga2pad'], P['gab'])


def _ab_out_kernel(yf_ref, yb_ref, r_ref, k_ref, v_ref, a0_ref, a1_ref, g_ref, of_ref, ob_ref, go_ref,
                   rk_ref, ka_ref, lnw_ref, lnb_ref, gn_ref, o_ref):
    y = yf_ref[...] + yb_ref[...]
    mean = _head_sum64(y) * (1.0 / HEAD_A)
    d = y - mean
    var = _head_sum64(d * d) * (1.0 / HEAD_A)
    yn = d * lax.rsqrt(var + RWKV_LN_EPS) * lnw_ref[...] + lnb_ref[...]
    kdsum = k_ref[...] * (2.0 + (a0_ref[...] + a1_ref[...] - 2.0) * ka_ref[...])
    bonus = _head_sum64(r_ref[...] * kdsum * rk_ref[...]) * v_ref[...]
    o_ref[:, :D_A] = ((yn + bonus) * g_ref[...]).astype(o_ref.dtype)
    o = of_ref[...] + ob_ref[...]
    go = go_ref[...]
    for h in range(H_B):
        sl = slice(h * DV_B, (h + 1) * DV_B)
        oh = o[:, sl]
        yh = oh * lax.rsqrt(jnp.mean(oh * oh, axis=-1, keepdims=True) + EPS) * gn_ref[...]
        gh = go[:, sl]
        o_ref[:, D_A + h * DV_B:D_A + (h + 1) * DV_B] = (yh * gh * _sigmoid(gh)).astype(o_ref.dtype)


def ab_out(yf, yb, prep, of, ob, p_all, P, n_lat):
    L = yf.shape[0]
    tm = _row_tile(n_lat, L, (128, 64, 32, 16, 8))
    rw = pl.BlockSpec((tm, D_A), lambda i: (i, 0))
    par = lambda w: pl.BlockSpec((1, w), lambda i: (0, 0))
    go_spec = pl.BlockSpec((tm, D_B_V), lambda i: (i, (3 * D_A + 2 * D_B_K + D_B_V) // D_B_V))
    return pl.pallas_call(
        _ab_out_kernel,
        grid=(L // tm,),
        in_specs=[rw] * 10 + [go_spec, par(D_A), par(D_A), par(D_A), par(D_A), par(DV_B)],
        out_specs=pl.BlockSpec((tm, D_MODEL), lambda i: (i, 0)),
        out_shape=jax.ShapeDtypeStruct((L, D_MODEL), BF16),
        compiler_params=_cparams(("parallel",)),
    )(yf, yb, prep['r'], prep['k'], prep['v'], prep['a'][0], prep['a'][1], prep['g'], of, ob, p_all,
      P['r_k'], P['k_a'], P['ln_w'], P['ln_b'], P['g_norm'])


def _mm_split3(m, x):
    hi = x.astype(BF16)
    r1 = x - hi.astype(F32)
    mid = r1.astype(BF16)
    lo = (r1 - mid.astype(F32)).astype(BF16)
    return _mm(m, hi) + _mm(m, mid) + _mm(m, lo)


def _chunk_block(c, n_lat_chunks, n_ctx_chunks, rev):
    if rev:
        return n_lat_chunks + n_ctx_chunks - 1 - c
    return jnp.where(c < n_ctx_chunks, n_lat_chunks + c, c - n_ctx_chunks)


def _rwkv_kernel(r_ref, lw_ref, k_ref, v_ref, kk_ref, a_ref, ka_ref, y_ref, s_sc, *, npairs, rev):
    C = SCAN_CHUNK
    C2 = 2 * C

    @pl.when(pl.program_id(1) == 0)
    def _():
        s_sc[...] = jnp.zeros(s_sc.shape, F32)

    row = lax.broadcasted_iota(jnp.int32, (C, C), 0)
    col = lax.broadcasted_iota(jnp.int32, (C, C), 1)
    tri = ((col >= row) if rev else (col <= row)).astype(BF16)
    row2 = lax.broadcasted_iota(jnp.int32, (C2, C2), 0)
    col2 = lax.broadcasted_iota(jnp.int32, (C2, C2), 1)
    same = (row2 // C) == (col2 // C)
    strict2 = same & ((col2 > row2) if rev else (col2 < row2))
    incl2 = same & ((col2 >= row2) if rev else (col2 <= row2))
    eye2 = (row2 == col2).astype(F32)
    last = 0 if rev else C - 1
    head0 = lax.broadcasted_iota(jnp.int32, (C, LANES), 1) < HEAD_A

    lw = lw_ref[...]
    cl = _mm_split3(tri, lw)
    cl_last = cl[last:last + 1, :]
    kk = kk_ref[...]
    a = a_ref[...]
    k = k_ref[...] * (1.0 + (a - 1.0) * ka_ref[...])
    b = a * kk
    p_inv = jnp.exp(-cl)
    tail = jnp.exp(cl_last - cl)
    kkd = kk * jnp.exp(cl - lw)
    rd = r_ref[...] * jnp.exp(cl)
    kinv = k * p_inv
    binv = b * p_inv
    kdec = k * tail
    nbdec = -(b * tail)
    v = v_ref[...]
    decay_last = jnp.exp(cl_last)

    def stack(x, p):
        xs = x[:, p * LANES:(p + 1) * LANES]
        return jnp.concatenate([jnp.where(head0, xs, 0.0), jnp.where(head0, 0.0, xs)], axis=0).astype(BF16)

    pairs = range(npairs)
    cat0 = lambda a_, b_: jnp.concatenate([a_, b_], axis=0)
    lhs = [cat0(stack(kkd, p), stack(rd, p)) for p in pairs]
    rhs_g = [cat0(stack(kinv, p), stack(binv, p)) for p in pairs]
    dec2 = [cat0(stack(kdec, p), stack(nbdec, p)) for p in pairs]
    v2 = [stack(v, p) for p in pairs]
    G = [_nt(lhs[p], rhs_g[p]) for p in pairs]
    a_kk = [jnp.where(strict2, G[p][:C2, :C2], 0.0).astype(BF16) for p in pairs]
    n1 = [jnp.where(strict2, -G[p][:C2, C2:], 0.0).astype(BF16) for p in pairs]
    b_rkb = [jnp.concatenate([jnp.where(incl2, G[p][C2:, :C2], 0.0),
                              jnp.where(incl2, -G[p][C2:, C2:], 0.0)], axis=1).astype(BF16) for p in pairs]

    powers = [n1]
    for _ in range(5):
        prev = powers[-1]
        powers.append([_mm(prev[p], prev[p]).astype(BF16) for p in pairs])
    tinv = [eye2 + n1[p].astype(F32) for p in pairs]
    for npow in powers[1:]:
        tinv = [tinv[p] + _mm(npow[p], tinv[p].astype(BF16)) for p in pairs]
    tinv = [t.astype(BF16) for t in tinv]

    S = [s_sc[p] for p in pairs]
    inter = [_nt(lhs[p], S[p].astype(BF16)) for p in pairs]
    rhs = [(inter[p][:C2] + _mm(a_kk[p], v2[p])).astype(BF16) for p in pairs]
    u2 = [_mm(tinv[p], rhs[p]).astype(BF16) for p in pairs]
    vu = [cat0(v2[p], u2[p]) for p in pairs]
    y2 = [inter[p][C2:] + _mm(b_rkb[p], vu[p]) for p in pairs]
    for p in pairs:
        sl = slice(p * LANES, (p + 1) * LANES)
        y_ref[:, sl] = y2[p][:C] + y2[p][C:]
        s_sc[p] = S[p] * decay_last[:, sl] + _tn(vu[p], dec2[p])


def rwkv_scan(r, lw, k, v, kk, a, k_a, n_lat, rev):
    L = r.shape[0]
    C = SCAN_CHUNK
    gw = 1024
    npairs = gw // LANES
    blk = functools.partial(_chunk_block, n_lat_chunks=n_lat // C, n_ctx_chunks=(L - n_lat) // C, rev=rev)
    spec = pl.BlockSpec((C, gw), lambda g, c: (blk(c), g))
    return pl.pallas_call(
        functools.partial(_rwkv_kernel, npairs=npairs, rev=rev),
        grid=(D_A // gw, L // C),
        in_specs=[spec] * 6 + [pl.BlockSpec((1, gw), lambda g, c: (0, g))],
        out_specs=spec,
        out_shape=jax.ShapeDtypeStruct((L, D_A), F32),
        scratch_shapes=[pltpu.VMEM((npairs, LANES, LANES), F32)],
        compiler_params=_cparams(("parallel", "arbitrary")),
    )(r, lw, k, v, kk, a, k_a)


def _gla_kernel(q_ref, k_ref, v_ref, g_ref, o_ref, s_sc, *, rev):
    C = SCAN_CHUNK

    @pl.when(pl.program_id(0) == 0)
    def _():
        s_sc[...] = jnp.zeros(s_sc.shape, F32)

    row = lax.broadcasted_iota(jnp.int32, (C, C), 0)
    col = lax.broadcasted_iota(jnp.int32, (C, C), 1)
    le = (col >= row) if rev else (col <= row)
    trow = lax.broadcasted_iota(jnp.int32, (C, 1), 0)
    mats = [jnp.where(le, 1.0, 0.0)]
    levels = []
    n = C
    while n >= 2:
        half = n // 2
        same = (row // n) == (col // n)
        row_2nd = ((row % n) >= half) != rev
        col_2nd = ((col % n) >= half) != rev
        mats.append(jnp.where(same & row_2nd & col_2nd & le, 1.0, 0.0)
                    - jnp.where(same & (~row_2nd) & (~col_2nd) & (~le), 1.0, 0.0))
        levels.append((same, ((trow % n) >= half) != rev))
        n = half
    mall = jnp.concatenate(mats, axis=0).astype(BF16)

    call = _mm_split3(mall, g_ref[...])
    b = call[:C]
    last = 0 if rev else C - 1
    b_last = b[last:last + 1, :]
    q = q_ref[...] * (DK_B ** -0.5)
    k = k_ref[...]
    qe = (q * jnp.exp(b)).astype(BF16)
    kd = (k * jnp.exp(b_last - b)).astype(BF16)
    dec = jnp.exp(b_last)
    qk = q * k
    qts, kts = [], []
    for l, (_, second) in enumerate(levels):
        e = jnp.exp(-jnp.abs(call[(l + 1) * C:(l + 2) * C]))
        qts.append(jnp.where(second, q * e, 0.0).astype(BF16))
        kts.append(jnp.where(second, 0.0, k * e).astype(BF16))
    vb = v_ref[...].astype(BF16)

    heads = range(H_B)
    ks = [slice(h * DK_B, (h + 1) * DK_B) for h in heads]
    vs = [slice(h * DV_B, (h + 1) * DV_B) for h in heads]
    St = [s_sc[h] for h in heads]
    o_inter = [_nt(qe[:, ks[h]], St[h].astype(BF16)) for h in heads]
    att = [jnp.where(row == col, jnp.sum(qk[:, ks[h]], axis=-1, keepdims=True), 0.0) for h in heads]
    for l, (same, _) in enumerate(levels):
        att = [att[h] + jnp.where(same, _nt(qts[l][:, ks[h]], kts[l][:, ks[h]]), 0.0) for h in heads]
    for h in heads:
        o_ref[:, vs[h]] = o_inter[h] + _mm(att[h].astype(BF16), vb[:, vs[h]])
        s_sc[h] = St[h] * dec[:, ks[h]] + _tn(vb[:, vs[h]], kd[:, ks[h]])


def gla_scan(p_all, g, n_lat, rev):
    L = p_all.shape[0]
    C = SCAN_CHUNK
    blk = functools.partial(_chunk_block, n_lat_chunks=n_lat // C, n_ctx_chunks=(L - n_lat) // C, rev=rev)
    qc, vc = 3 * D_A // D_B_K, (3 * D_A + 2 * D_B_K) // D_B_V
    return pl.pallas_call(
        functools.partial(_gla_kernel, rev=rev),
        grid=(L // C,),
        in_specs=[pl.BlockSpec((C, D_B_K), lambda c: (blk(c), qc)),
                  pl.BlockSpec((C, D_B_K), lambda c: (blk(c), qc + 1)),
                  pl.BlockSpec((C, D_B_V), lambda c: (blk(c), vc)),
                  pl.BlockSpec((C, D_B_K), lambda c: (blk(c), 0))],
        out_specs=pl.BlockSpec((C, D_B_V), lambda c: (blk(c), 0)),
        out_shape=jax.ShapeDtypeStruct((L, D_B_V), F32),
        scratch_shapes=[pltpu.VMEM((H_B, DV_B, DK_B), F32)],
        compiler_params=_cparams(("arbitrary",)),
    )(p_all, p_all, p_all, g)


def _seg(m, j):
    return m[0:2, j * D_MODEL:(j + 1) * D_MODEL]


def _coef(rows2):
    return jnp.zeros((SUBLANES, D_MODEL), F32).at[0:2].set(rows2)


def _ffn_half(h, m, kidx, gain, w13, w2, n_lat, n_rows):
    shift, scale, gate = _seg(m, 3 * kidx), _seg(m, 3 * kidx + 1), _seg(m, 3 * kidx + 2)
    u = normmod(h, (gain[None, :] * (1.0 + scale))[:, None, :], shift[:, None, :], n_lat, n_rows, BF16)
    act = matmul_swiglu(u, w13)
    return matmul_residual(act, w2, h, _coef(FFN_RES * gate), n_lat)


def _shifted_lora(w, mu):
    return jnp.concatenate([(1.0 - mu[0] - mu[1])[:, None] * w, mu[0][:, None] * w, mu[1][:, None] * w], axis=1)


def _mixer_ab(u, P, n_lat):
    p_all = matmul(u, P['w_big'])
    r, k, v, kk, lw0, lw1, a0, a1, g, la0, la1 = ab_prep(p_all, P, n_lat)
    prep = dict(r=r, k=k, v=v, a=(a0, a1), g=g)
    yf = rwkv_scan(r, lw0, k, v, kk, a0, P['k_a'], n_lat, False)
    yb = rwkv_scan(r, lw1, k, v, kk, a1, P['k_a'], n_lat, True)
    of = gla_scan(p_all, la0, n_lat, False)
    ob = gla_scan(p_all, la1, n_lat, True)
    return ab_out(yf, yb, prep, of, ob, p_all, P, n_lat)


def _rope_tables(n_lat, n_rows):
    rows = n_lat // GRID_W
    rr = jnp.repeat(jnp.arange(rows), GRID_W)
    cc = jnp.tile(jnp.arange(GRID_W), rows)
    inv_freq = ROPE_THETA ** (-jnp.arange(ROPE_PAIRS, dtype=F32) / ROPE_PAIRS)
    ar = rr.astype(F32)[:, None] * inv_freq
    ac = cc.astype(F32)[:, None] * inv_freq
    cos = jnp.concatenate([jnp.cos(ar), jnp.cos(ar), jnp.cos(ac), jnp.cos(ac)], axis=1)
    sin = jnp.concatenate([-jnp.sin(ar), jnp.sin(ar), -jnp.sin(ac), jnp.sin(ac)], axis=1)
    n_ctx = n_rows - n_lat
    cos = jnp.concatenate([cos, jnp.ones((n_ctx, HEAD_C), F32)], axis=0)
    sin = jnp.concatenate([sin, jnp.zeros((n_ctx, HEAD_C), F32)], axis=0)
    return cos, sin


def _mixer_c(u, P, n_lat, need_ctx):
    L = u.shape[0]
    p = matmul(u, P['w_in'])
    q, k, v_ext = attn_prep(p, P['cos'], P['sin'], P['q_gain'], P['k_gain'])
    o_l = flash_gqa(q, k, v_ext, (0, n_lat), (0, L))
    if not need_ctx:
        return o_l
    o_c = flash_gqa(q, k, v_ext, (n_lat, L - n_lat), (n_lat, L - n_lat))
    return jnp.concatenate([o_l, o_c], axis=0)


def kernel(x, c, ctx, c_ctx, ada_down, ada_up, ada_bias, norm_gains, final_gain, ffn_w13, ffn_w2, ab_w_in, ab_w_out, rwkv_mu_rkv, rwkv_mu_lora, rwkv_w1, rwkv_w2, rwkv_w0, rwkv_a1, rwkv_a2, rwkv_a0, rwkv_g1, rwkv_g2, rwkv_k_k, rwkv_k_a, rwkv_r_k, rwkv_ln_w, rwkv_ln_b, gla_a1, gla_a2, gla_ab, gla_norm, attn_w_in, attn_w_out, attn_q_norm, attn_k_norm):
    D = D_MODEL
    n_lat = x.shape[1]
    h = jnp.concatenate([x[0], ctx[0]], axis=0)
    n_rows = h.shape[0]
    bf = lambda t: t.astype(BF16)
    cvec = jnp.zeros((16, D), F32).at[0].set(c[0]).at[1].set(c_ctx)
    cs = bf(cvec * jax.nn.sigmoid(cvec))
    cos, sin = _rope_tables(n_lat, n_rows)
    cat1 = lambda ts: jnp.concatenate(ts, axis=1)
    for layer in range(DEPTH):
        last = layer == DEPTH - 1
        i = layer // 2
        m = matmul(bf(matmul(cs, bf(ada_down[layer]))), bf(ada_up[layer])) + ada_bias[layer]
        gains = norm_gains[layer]
        h = _ffn_half(h, m, 0, gains[0], bf(ffn_w13[layer, 0]), bf(ffn_w2[layer, 0]), n_lat, n_rows)
        u = normmod(h, (gains[1][None] * (1.0 + _seg(m, 4)))[:, None, :], _seg(m, 3)[:, None, :], n_lat, n_rows, BF16)
        if layer % 2 == 0:
            mu = rwkv_mu_lora[i]
            rg = R_GLA_GATE
            lora = cat1([_shifted_lora(cat1([rwkv_w1[i, 0], rwkv_w1[i, 1]]), mu[0]),
                         _shifted_lora(cat1([rwkv_a1[i, 0], rwkv_a1[i, 1]]), mu[1]),
                         _shifted_lora(rwkv_g1[i], mu[2]),
                         gla_a1[i, 0], gla_a1[i, 1]])
            lora = jnp.pad(lora, ((0, 0), (0, LORA_W - lora.shape[1])))
            ga2pad = jnp.stack([jnp.pad(gla_a2[i, d], ((d * rg, LANES - (d + 1) * rg), (0, 0))) for d in range(2)])
            P = dict(
                w_big=bf(cat1([ab_w_in[i], lora])), mu_rkv=rwkv_mu_rkv[i],
                w2=bf(rwkv_w2[i]), w0=rwkv_w0[i], a2=bf(rwkv_a2[i]), a0=rwkv_a0[i], g2=bf(rwkv_g2[i]),
                k_k=rwkv_k_k[i][None], k_a=rwkv_k_a[i][None], r_k=rwkv_r_k[i].reshape(1, D_A),
                ln_w=rwkv_ln_w[i][None], ln_b=rwkv_ln_b[i][None],
                ga2pad=bf(ga2pad), gab=gla_ab[i], g_norm=gla_norm[i][None])
            y = _mixer_ab(u, P, n_lat)
            w_out = bf(ab_w_out[i])
        else:
            P = dict(w_in=bf(attn_w_in[i]), q_gain=attn_q_norm[i][None], k_gain=attn_k_norm[i][None], cos=cos, sin=sin)
            y = _mixer_c(u, P, n_lat, not last)
            w_out = bf(attn_w_out[i])
        if last:
            n_rows = n_lat
        h = matmul_residual(y, w_out, h, _coef(_seg(m, 5)), n_lat)
        h = _ffn_half(h, m, 2, gains[2], bf(ffn_w13[layer, 1]), bf(ffn_w2[layer, 1]), n_lat, n_rows)
    ones2 = jnp.stack([final_gain, final_gain])[:, None, :]
    out = normmod(h, ones2, jnp.zeros((2, 1, D), F32), n_lat, n_lat, F32)
    return out[None]
```

```python
import functools
import math

import jax
import jax.numpy as jnp
from jax import lax
from jax.experimental import pallas as pl
from jax.experimental.pallas import tpu as pltpu

F32 = jnp.float32
BF16 = jnp.bfloat16

D_MODEL = 4096
DEPTH = 4
GRID_W = 64
N_MOD = 9
D_FF = 6144
FFN_RES = 0.5
EPS = 1e-6
D_A = D_MODEL // 2
HEAD_A = 64
H_A = D_A // HEAD_A
R_DECAY = 128
R_AAA = 128
R_GATE = 256
RWKV_LN_EPS = 64e-5
D_B_K = D_MODEL // 4
D_B_V = D_MODEL // 2
H_B = 4
DK_B = D_B_K // H_B
DV_B = D_B_V // H_B
R_GLA_GATE = 16
GLA_GATE_NORM = 16.0
D_AB_IN = 3 * D_A + 2 * D_B_K + 2 * D_B_V
HEAD_C = 128
H_C = D_MODEL // HEAD_C
KV_C = 8
G_C = H_C // KV_C
KV_DIM = KV_C * HEAD_C
ROPE_PAIRS = HEAD_C // 4
ROPE_THETA = 10000.0
LOG2E = 1.4426950408889634

LANES = 128
SUBLANES = 8
SCAN_CHUNK = 64
VMEM_LIMIT = 56 * 1024 * 1024

LORA_W = 3072
AB_COLS = D_AB_IN + LORA_W
OFF_ZW, OFF_ZA, OFF_ZG = 0, 3 * 2 * R_DECAY, 3 * 2 * R_DECAY + 3 * 2 * R_AAA
OFF_ZGA = OFF_ZG + 3 * R_GATE


def _pick(n, prefs):
    for p in prefs:
        if n % p == 0:
            return p
    return n


def _cparams(sem):
    return pltpu.CompilerParams(dimension_semantics=sem, vmem_limit_bytes=VMEM_LIMIT)


def _row_tile(n_lat, n_rows, prefs):
    return _pick(math.gcd(n_lat, n_rows - n_lat) if n_rows > n_lat else n_lat, prefs)


def _sigmoid(x):
    return 1.0 / (1.0 + jnp.exp(-x))


def _softplus(x):
    return jnp.maximum(x, 0.0) + jnp.log(1.0 + jnp.exp(-jnp.abs(x)))


def _nt(a, b):
    return lax.dot_general(a, b, (((1,), (1,)), ((), ())), preferred_element_type=F32)


def _tn(a, b):
    return lax.dot_general(a, b, (((0,), (0,)), ((), ())), preferred_element_type=F32)


def _mm(a, b):
    return jnp.dot(a, b, preferred_element_type=F32)


def _mm_kernel(x_ref, w_ref, o_ref):
    o_ref[...] = _mm(x_ref[...], w_ref[...]).astype(o_ref.dtype)


def _mm_swiglu_kernel(x_ref, wg_ref, wu_ref, o_ref):
    x = x_ref[...]
    g = _mm(x, wg_ref[...])
    u = _mm(x, wu_ref[...])
    o_ref[...] = (g * _sigmoid(g) * u).astype(o_ref.dtype)


def _mm_res_kernel(x_ref, w_ref, h_ref, c_ref, o_ref, *, n_lat):
    tm = x_ref.shape[0]
    rows = pl.program_id(0) * tm + lax.broadcasted_iota(jnp.int32, (tm, 1), 0)
    coef = jnp.where(rows >= n_lat, c_ref[1:2, :], c_ref[0:1, :])
    o_ref[...] = h_ref[...] + coef * _mm(x_ref[...], w_ref[...])


_TM_PREFS = (1280, 1024, 640, 512, 256, 128, 64, 32, 16)


def _wspec(w, lead, tn, col):
    lead = tuple(lead)
    K = w.shape[-2]
    return pl.BlockSpec((None,) * len(lead) + (K, tn), lambda i, j: lead + (0, col(j)))


def matmul(x, w, lead=(), out_dtype=F32):
    M, K = x.shape
    N = w.shape[-1]
    tm = _pick(M, _TM_PREFS)
    wide = tm * K * 2 <= 8 * 1024 * 1024
    tn = _pick(N, (1024, 512, 256, 128) if wide else (512, 256, 128))
    return pl.pallas_call(
        _mm_kernel,
        grid=(M // tm, N // tn),
        in_specs=[pl.BlockSpec((tm, K), lambda i, j: (i, 0)),
                  _wspec(w, lead, tn, lambda j: j)],
        out_specs=pl.BlockSpec((tm, tn), lambda i, j: (i, j)),
        out_shape=jax.ShapeDtypeStruct((M, N), out_dtype),
        compiler_params=_cparams(("parallel", "parallel")),
    )(x, w)


def matmul_swiglu(x, w13, lead=()):
    M, K = x.shape
    F = w13.shape[-1] // 2
    tm = _pick(M, _TM_PREFS)
    tn = _pick(F, (512, 256, 128))
    nf = F // tn
    return pl.pallas_call(
        _mm_swiglu_kernel,
        grid=(M // tm, nf),
        in_specs=[pl.BlockSpec((tm, K), lambda i, j: (i, 0)),
                  _wspec(w13, lead, tn, lambda j: j),
                  _wspec(w13, lead, tn, lambda j: j + nf)],
        out_specs=pl.BlockSpec((tm, tn), lambda i, j: (i, j)),
        out_shape=jax.ShapeDtypeStruct((M, F), BF16),
        compiler_params=_cparams(("parallel", "parallel")),
    )(x, w13, w13)


def matmul_residual(x, w, h, coef, n_lat, lead=()):
    M, K = x.shape
    N = w.shape[-1]
    tm = _pick(M, _TM_PREFS)
    tn = _pick(N, (512, 256, 128))
    return pl.pallas_call(
        functools.partial(_mm_res_kernel, n_lat=n_lat),
        grid=(M // tm, N // tn),
        in_specs=[pl.BlockSpec((tm, K), lambda i, j: (i, 0), pipeline_mode=pl.Buffered(1)),
                  _wspec(w, lead, tn, lambda j: j),
                  pl.BlockSpec((tm, tn), lambda i, j: (i, j)),
                  pl.BlockSpec((SUBLANES, tn), lambda i, j: (0, j))],
        out_specs=pl.BlockSpec((tm, tn), lambda i, j: (i, j)),
        out_shape=jax.ShapeDtypeStruct((M, N), F32),
        compiler_params=_cparams(("parallel", "parallel")),
    )(x, w, h, coef)


def _normmod_kernel(h_ref, g_ref, s_ref, o_ref):
    x = h_ref[...]
    y = x * lax.rsqrt(jnp.mean(x * x, axis=-1, keepdims=True) + EPS)
    o_ref[...] = (y * g_ref[...] + s_ref[...]).astype(o_ref.dtype)


def normmod(h, g, s, n_lat, n_rows, out_dtype):
    D = h.shape[1]
    tm = _row_tile(n_lat, n_rows, (256, 128, 64, 32, 16, 8))
    seg = lambda i: (jnp.where(i * tm >= n_lat, 1, 0), 0, 0)
    return pl.pallas_call(
        _normmod_kernel,
        grid=(n_rows // tm,),
        in_specs=[pl.BlockSpec((tm, D), lambda i: (i, 0)),
                  pl.BlockSpec((None, 1, D), seg),
                  pl.BlockSpec((None, 1, D), seg)],
        out_specs=pl.BlockSpec((tm, D), lambda i: (i, 0)),
        out_shape=jax.ShapeDtypeStruct((n_rows, D), out_dtype),
        compiler_params=_cparams(("parallel",)),
    )(h, g, s)


def _flash_kernel(q_ref, k_ref, v_ref, o_ref, m_sc, acc_sc):
    j = pl.program_id(2)
    reps = k_ref.shape[0] // LANES

    @pl.when(j == 0)
    def _():
        m_sc[...] = jnp.full(m_sc.shape, -1e30, F32)
        acc_sc[...] = jnp.zeros(acc_sc.shape, F32)

    k = k_ref[...]
    v = v_ref[...]
    heads = range(G_C)
    tq = q_ref.shape[0]
    rh = min(tq, 512)
    units = [(g, r) for r in range(0, tq, rh) for g in heads]

    def scores(u):
        g, r = u
        return _nt(q_ref[r:r + rh, g * HEAD_C:(g + 1) * HEAD_C], k)

    def fold(u, s):
        g, r = u
        m_prev = m_sc[g, r:r + rh, :]
        m_new = jnp.maximum(m_prev, jnp.max(s, axis=-1, keepdims=True))
        p = jnp.exp2((s - jnp.concatenate([m_new] * reps, axis=1)).astype(BF16))
        alpha = jnp.exp2(m_prev - m_new)
        acc_sc[g, r:r + rh, :] = jnp.concatenate([alpha, alpha], axis=1) * acc_sc[g, r:r + rh, :] + _mm(p, v)
        m_sc[g, r:r + rh, :] = m_new

    ahead = 2
    pending = [scores(u) for u in units[:ahead]]
    for idx, u in enumerate(units):
        if idx + ahead < len(units):
            pending.append(scores(units[idx + ahead]))
        fold(u, pending[idx])

    @pl.when(j == pl.num_programs(2) - 1)
    def _():
        for g in heads:
            a = acc_sc[g]
            o_ref[:, g * HEAD_C:(g + 1) * HEAD_C] = (a[:, :HEAD_C] / a[:, HEAD_C:]).astype(o_ref.dtype)


def flash_gqa(q, k, v_ext, q_rows, k_rows):
    q0, Lq = q_rows
    k0, Lk = k_rows
    tq = _pick(math.gcd(q0, Lq), (1024, 512, 256, 128))
    tk = _pick(math.gcd(k0, Lk), (1280, 1024, 640, 512, 256, 128))
    qb, kb = q0 // tq, k0 // tk
    gw = G_C * HEAD_C
    return pl.pallas_call(
        _flash_kernel,
        grid=(KV_C, Lq // tq, Lk // tk),
        in_specs=[pl.BlockSpec((tq, gw), lambda n, i, j: (i + qb, n)),
                  pl.BlockSpec((tk, HEAD_C), lambda n, i, j: (j + kb, n)),
                  pl.BlockSpec((tk, 2 * HEAD_C), lambda n, i, j: (j + kb, n))],
        out_specs=pl.BlockSpec((tq, gw), lambda n, i, j: (i, n)),
        out_shape=jax.ShapeDtypeStruct((Lq, H_C * HEAD_C), BF16),
        scratch_shapes=[pltpu.VMEM((G_C, tq, LANES), F32),
                        pltpu.VMEM((G_C, tq, 2 * HEAD_C), F32)],
        compiler_params=_cparams(("parallel", "parallel", "arbitrary")),
    )(q, k, v_ext)


def _attn_prep_kernel(p_ref, cos_ref, sin_ref, qg_ref, kg_ref, q_ref, k_ref, v_ref):
    cos = cos_ref[...]
    sin = sin_ref[...]
    lane = lax.broadcasted_iota(jnp.int32, cos.shape, 1)
    first = (lane % (2 * ROPE_PAIRS)) < ROPE_PAIRS

    def norm_rope(x, gain):
        xn = x * lax.rsqrt(jnp.mean(x * x, axis=-1, keepdims=True) + EPS) * gain
        partner = jnp.where(first, pltpu.roll(xn, HEAD_C - ROPE_PAIRS, 1), pltpu.roll(xn, ROPE_PAIRS, 1))
        return xn * cos + partner * sin

    qg = qg_ref[...] * (LOG2E * HEAD_C ** -0.5)
    kg = kg_ref[...]
    for h in range(H_C):
        sl = slice(h * HEAD_C, (h + 1) * HEAD_C)
        q_ref[:, sl] = norm_rope(p_ref[:, sl], qg).astype(q_ref.dtype)
    ones = jnp.ones((p_ref.shape[0], HEAD_C), v_ref.dtype)
    for h in range(KV_C):
        sl = slice(h * HEAD_C, (h + 1) * HEAD_C)
        k_ref[:, sl] = norm_rope(p_ref[:, D_MODEL + h * HEAD_C:D_MODEL + (h + 1) * HEAD_C], kg).astype(k_ref.dtype)
        v_ref[:, 2 * h * HEAD_C:(2 * h + 1) * HEAD_C] = \
            p_ref[:, D_MODEL + KV_DIM + h * HEAD_C:D_MODEL + KV_DIM + (h + 1) * HEAD_C].astype(v_ref.dtype)
        v_ref[:, (2 * h + 1) * HEAD_C:(2 * h + 2) * HEAD_C] = ones


def attn_prep(p, cos, sin, q_gain, k_gain):
    L = p.shape[0]
    tm = _pick(L, (256, 128, 64, 32, 16, 8))
    row = lambda w: pl.BlockSpec((tm, w), lambda i: (i, 0))
    par = pl.BlockSpec((1, HEAD_C), lambda i: (0, 0))
    return pl.pallas_call(
        _attn_prep_kernel,
        grid=(L // tm,),
        in_specs=[row(D_MODEL + 2 * KV_DIM), row(HEAD_C), row(HEAD_C), par, par],
        out_specs=[row(D_MODEL), row(KV_DIM), row(2 * KV_DIM)],
        out_shape=[jax.ShapeDtypeStruct((L, D_MODEL), BF16),
                   jax.ShapeDtypeStruct((L, KV_DIM), BF16),
                   jax.ShapeDtypeStruct((L, 2 * KV_DIM), BF16)],
        compiler_params=_cparams(("parallel",)),
    )(p, cos, sin, q_gain, k_gain)


def _head_sum64(x):
    r_ = lax.broadcasted_iota(jnp.int32, (LANES, LANES), 0) // HEAD_A
    c_ = lax.broadcasted_iota(jnp.int32, (LANES, LANES), 1) // HEAD_A
    ones_bd = (r_ == c_).astype(BF16)
    hi = x.astype(BF16)
    lo = (x - hi.astype(F32)).astype(BF16)
    tiles = [_mm(hi[:, j:j + LANES], ones_bd) + _mm(lo[:, j:j + LANES], ones_bd) for j in range(0, x.shape[1], LANES)]
    return jnp.concatenate(tiles, axis=1)


def _ab_prep_kernel(x_ref, xp_ref, xn_ref, z_ref, zp_ref, zn_ref, mu_ref, w2_ref, w0_ref, a2_ref, a0_ref,
                    g2_ref, kk_ref, ga2_ref, gab_ref,
                    r_o, k_o, v_o, kkn_o, lw0_o, lw1_o, a0_o, a1_o, g_o, la0_o, la1_o, *, n_lat, n_rows):
    tm = x_ref.shape[0]
    i = pl.program_id(0)
    loc = lax.broadcasted_iota(jnp.int32, (tm, 1), 0)
    rows = i * tm + loc
    seg_first = (rows == 0) | (rows == n_lat)
    seg_last = (rows == n_lat - 1) | (rows == n_rows - 1)

    def prev(x, halo):
        y = jnp.where(loc == 0, halo[SUBLANES - 1:SUBLANES, :], pltpu.roll(x, 1, 0))
        return jnp.where(seg_first, 0.0, y)

    def nxt(x, halo):
        y = jnp.where(loc == tm - 1, halo[0:1, :], pltpu.roll(x, tm - 1, 0))
        return jnp.where(seg_last, 0.0, y)

    x = x_ref[...]
    mu = mu_ref[...]
    rkv = x + mu[0:1, :] * (prev(x, xp_ref[...]) - x) + mu[1:2, :] * (nxt(x, xn_ref[...]) - x)
    r_o[...] = rkv[:, :D_A]
    k = rkv[:, D_A:2 * D_A]
    k_o[...] = k
    v_o[...] = rkv[:, 2 * D_A:]
    kk = k * kk_ref[...]
    kkn_o[...] = kk / jnp.maximum(jnp.sqrt(_head_sum64(kk * kk)), 1e-12)

    def lora(off, width):
        c = z_ref[:, off:off + width]
        p_ = prev(z_ref[:, off + width:off + 2 * width], zp_ref[:, off + width:off + 2 * width])
        n_ = nxt(z_ref[:, off + 2 * width:off + 3 * width], zn_ref[:, off + 2 * width:off + 3 * width])
        return c + p_ + n_

    zw = jnp.tanh(lora(OFF_ZW, 2 * R_DECAY)).astype(BF16)
    za = lora(OFF_ZA, 2 * R_AAA).astype(BF16)
    zg = _sigmoid(lora(OFF_ZG, R_GATE)).astype(BF16)
    zga = z_ref[:, OFF_ZGA:OFF_ZGA + LANES].astype(BF16)
    g_o[...] = _mm(zg, g2_ref[...])
    for d, (lw_o, a_o, la_o) in enumerate(((lw0_o, a0_o, la0_o), (lw1_o, a1_o, la1_o))):
        lw = _mm(zw[:, d * R_DECAY:(d + 1) * R_DECAY], w2_ref[d]) + w0_ref[d:d + 1, :]
        lw_o[...] = -jnp.exp(-_softplus(-lw) - 0.5)
        a_o[...] = _sigmoid(_mm(za[:, d * R_AAA:(d + 1) * R_AAA], a2_ref[d]) + a0_ref[d:d + 1, :])
        zz = _mm(zga, ga2_ref[d]) + gab_ref[d:d + 1, :]
        la_o[...] = -_softplus(-zz) * (1.0 / GLA_GATE_NORM)


def ab_prep(p_all, P, n_lat):
    L = p_all.shape[0]
    tm = _row_tile(n_lat, L, (128, 64, 32, 16, 8))
    nb8 = tm // SUBLANES
    last8 = L // SUBLANES - 1
    wx, cz = 3 * D_A, D_AB_IN // LORA_W
    full = lambda shape: pl.BlockSpec(shape, lambda i: (0,) * len(shape))
    outw = lambda w: pl.BlockSpec((tm, w), lambda i: (i, 0))
    in_specs = [
        pl.BlockSpec((tm, wx), lambda i: (i, 0)),
        pl.BlockSpec((SUBLANES, wx), lambda i: (jnp.maximum(i * nb8 - 1, 0), 0)),
        pl.BlockSpec((SUBLANES, wx), lambda i: (jnp.minimum((i + 1) * nb8, last8), 0)),
        pl.BlockSpec((tm, LORA_W), lambda i: (i, cz)),
        pl.BlockSpec((SUBLANES, LORA_W), lambda i: (jnp.maximum(i * nb8 - 1, 0), cz)),
        pl.BlockSpec((SUBLANES, LORA_W), lambda i: (jnp.minimum((i + 1) * nb8, last8), cz)),
        full((2, wx)), full((2, R_DECAY, D_A)), full((2, D_A)), full((2, R_AAA, D_A)), full((2, D_A)),
        full((R_GATE, D_A)), full((1, D_A)), full((2, LANES, D_B_K)), full((2, D_B_K)),
    ]
    outs = [D_A] * 9 + [D_B_K] * 2
    return pl.pallas_call(
        functools.partial(_ab_prep_kernel, n_lat=n_lat, n_rows=L),
        grid=(L // tm,),
        in_specs=in_specs,
        out_specs=[outw(w) for w in outs],
        out_shape=[jax.ShapeDtypeStruct((L, w), F32) for w in outs],
        compiler_params=_cparams(("parallel",)),
    )(p_all, p_all, p_all, p_all, p_all, p_all, P['mu_rkv'], P['w2'], P['w0'], P['a2'], P['a0'],
      P['g2'], P['k_k'], P['ga2pad'], P['gab'])


def _ab_out_kernel(yf_ref, yb_ref, r_ref, k_ref, v_ref, a0_ref, a1_ref, g_ref, of_ref, ob_ref, go_ref,
                   rk_ref, ka_ref, lnw_ref, lnb_ref, gn_ref, o_ref):
    y = yf_ref[...] + yb_ref[...]
    mean = _head_sum64(y) * (1.0 / HEAD_A)
    d = y - mean
    var = _head_sum64(d * d) * (1.0 / HEAD_A)
    yn = d * lax.rsqrt(var + RWKV_LN_EPS) * lnw_ref[...] + lnb_ref[...]
    kdsum = k_ref[...] * (2.0 + (a0_ref[...] + a1_ref[...] - 2.0) * ka_ref[...])
    bonus = _head_sum64(r_ref[...] * kdsum * rk_ref[...]) * v_ref[...]
    o_ref[:, :D_A] = ((yn + bonus) * g_ref[...]).astype(o_ref.dtype)
    o = of_ref[...] + ob_ref[...]
    go = go_ref[...]
    for h in range(H_B):
        sl = slice(h * DV_B, (h + 1) * DV_B)
        oh = o[:, sl]
        yh = oh * lax.rsqrt(jnp.mean(oh * oh, axis=-1, keepdims=True) + EPS) * gn_ref[...]
        gh = go[:, sl]
        o_ref[:, D_A + h * DV_B:D_A + (h + 1) * DV_B] = (yh * gh * _sigmoid(gh)).astype(o_ref.dtype)


def ab_out(yf, yb, prep, of, ob, p_all, P, n_lat):
    L = yf.shape[0]
    tm = _row_tile(n_lat, L, (128, 64, 32, 16, 8))
    rw = pl.BlockSpec((tm, D_A), lambda i: (i, 0))
    par = lambda w: pl.BlockSpec((1, w), lambda i: (0, 0))
    go_spec = pl.BlockSpec((tm, D_B_V), lambda i: (i, (3 * D_A + 2 * D_B_K + D_B_V) // D_B_V))
    return pl.pallas_call(
        _ab_out_kernel,
        grid=(L // tm,),
        in_specs=[rw] * 10 + [go_spec, par(D_A), par(D_A), par(D_A), par(D_A), par(DV_B)],
        out_specs=pl.BlockSpec((tm, D_MODEL), lambda i: (i, 0)),
        out_shape=jax.ShapeDtypeStruct((L, D_MODEL), BF16),
        compiler_params=_cparams(("parallel",)),
    )(yf, yb, prep['r'], prep['k'], prep['v'], prep['a'][0], prep['a'][1], prep['g'], of, ob, p_all,
      P['r_k'], P['k_a'], P['ln_w'], P['ln_b'], P['g_norm'])


def _mm_split3(m, x):
    hi = x.astype(BF16)
    r1 = x - hi.astype(F32)
    mid = r1.astype(BF16)
    lo = (r1 - mid.astype(F32)).astype(BF16)
    return _mm(m, hi) + _mm(m, mid) + _mm(m, lo)


def _chunk_block(c, n_lat_chunks, n_ctx_chunks, rev):
    if rev:
        return n_lat_chunks + n_ctx_chunks - 1 - c
    return jnp.where(c < n_ctx_chunks, n_lat_chunks + c, c - n_ctx_chunks)


def _rwkv_kernel(r_ref, lw_ref, k_ref, v_ref, kk_ref, a_ref, ka_ref, y_ref, s_sc, *, npairs, rev):
    C = SCAN_CHUNK
    C2 = 2 * C

    @pl.when(pl.program_id(1) == 0)
    def _():
        s_sc[...] = jnp.zeros(s_sc.shape, F32)

    row = lax.broadcasted_iota(jnp.int32, (C, C), 0)
    col = lax.broadcasted_iota(jnp.int32, (C, C), 1)
    tri = ((col >= row) if rev else (col <= row)).astype(BF16)
    row2 = lax.broadcasted_iota(jnp.int32, (C2, C2), 0)
    col2 = lax.broadcasted_iota(jnp.int32, (C2, C2), 1)
    same = (row2 // C) == (col2 // C)
    strict2 = same & ((col2 > row2) if rev else (col2 < row2))
    incl2 = same & ((col2 >= row2) if rev else (col2 <= row2))
    eye2 = (row2 == col2).astype(F32)
    last = 0 if rev else C - 1
    head0 = lax.broadcasted_iota(jnp.int32, (C, LANES), 1) < HEAD_A

    lw = lw_ref[...]
    cl = _mm_split3(tri, lw)
    cl_last = cl[last:last + 1, :]
    kk = kk_ref[...]
    a = a_ref[...]
    k = k_ref[...] * (1.0 + (a - 1.0) * ka_ref[...])
    b = a * kk
    p_inv = jnp.exp(-cl)
    tail = jnp.exp(cl_last - cl)
    kkd = kk * jnp.exp(cl - lw)
    rd = r_ref[...] * jnp.exp(cl)
    kinv = k * p_inv
    binv = b * p_inv
    kdec = k * tail
    nbdec = -(b * tail)
    v = v_ref[...]
    decay_last = jnp.exp(cl_last)

    def stack(x, p):
        xs = x[:, p * LANES:(p + 1) * LANES]
        return jnp.concatenate([jnp.where(head0, xs, 0.0), jnp.where(head0, 0.0, xs)], axis=0).astype(BF16)

    pairs = range(npairs)
    cat0 = lambda a_, b_: jnp.concatenate([a_, b_], axis=0)
    lhs = [cat0(stack(kkd, p), stack(rd, p)) for p in pairs]
    rhs_g = [cat0(stack(kinv, p), stack(binv, p)) for p in pairs]
    dec2 = [cat0(stack(kdec, p), stack(nbdec, p)) for p in pairs]
    v2 = [stack(v, p) for p in pairs]
    G = [_nt(lhs[p], rhs_g[p]) for p in pairs]
    a_kk = [jnp.where(strict2, G[p][:C2, :C2], 0.0).astype(BF16) for p in pairs]
    n1 = [jnp.where(strict2, -G[p][:C2, C2:], 0.0).astype(BF16) for p in pairs]
    b_rkb = [jnp.concatenate([jnp.where(incl2, G[p][C2:, :C2], 0.0),
                              jnp.where(incl2, -G[p][C2:, C2:], 0.0)], axis=1).astype(BF16) for p in pairs]

    powers = [n1]
    for _ in range(5):
        prev = powers[-1]
        powers.append([_mm(prev[p], prev[p]).astype(BF16) for p in pairs])
    tinv = [eye2 + n1[p].astype(F32) for p in pairs]
    for npow in powers[1:]:
        tinv = [tinv[p] + _mm(npow[p], tinv[p].astype(BF16)) for p in pairs]
    tinv = [t.astype(BF16) for t in tinv]

    S = [s_sc[p] for p in pairs]
    inter = [_nt(lhs[p], S[p].astype(BF16)) for p in pairs]
    rhs = [inter[p][:C2] + _mm(a_kk[p], v2[p]) for p in pairs]
    uf = [_mm(tinv[p], rhs[p].astype(BF16)) for p in pairs]
    res = [rhs[p] - uf[p] + _mm(n1[p], uf[p].astype(BF16)) for p in pairs]
    u2 = [(uf[p] + _mm(tinv[p], res[p].astype(BF16))).astype(BF16) for p in pairs]
    vu = [cat0(v2[p], u2[p]) for p in pairs]
    y2 = [inter[p][C2:] + _mm(b_rkb[p], vu[p]) for p in pairs]
    for p in pairs:
        sl = slice(p * LANES, (p + 1) * LANES)
        y_ref[:, sl] = y2[p][:C] + y2[p][C:]
        s_sc[p] = S[p] * decay_last[:, sl] + _tn(vu[p], dec2[p])


def rwkv_scan(r, lw, k, v, kk, a, k_a, n_lat, rev):
    L = r.shape[0]
    C = SCAN_CHUNK
    gw = 1024
    npairs = gw // LANES
    blk = functools.partial(_chunk_block, n_lat_chunks=n_lat // C, n_ctx_chunks=(L - n_lat) // C, rev=rev)
    spec = pl.BlockSpec((C, gw), lambda g, c: (blk(c), g))
    return pl.pallas_call(
        functools.partial(_rwkv_kernel, npairs=npairs, rev=rev),
        grid=(D_A // gw, L // C),
        in_specs=[spec] * 6 + [pl.BlockSpec((1, gw), lambda g, c: (0, g))],
        out_specs=spec,
        out_shape=jax.ShapeDtypeStruct((L, D_A), F32),
        scratch_shapes=[pltpu.VMEM((npairs, LANES, LANES), F32)],
        compiler_params=_cparams(("parallel", "arbitrary")),
    )(r, lw, k, v, kk, a, k_a)


def _gla_kernel(q_ref, k_ref, v_ref, g_ref, o_ref, s_sc, *, rev):
    C = SCAN_CHUNK

    @pl.when(pl.program_id(0) == 0)
    def _():
        s_sc[...] = jnp.zeros(s_sc.shape, F32)

    row = lax.broadcasted_iota(jnp.int32, (C, C), 0)
    col = lax.broadcasted_iota(jnp.int32, (C, C), 1)
    le = (col >= row) if rev else (col <= row)
    trow = lax.broadcasted_iota(jnp.int32, (C, 1), 0)
    mats = [jnp.where(le, 1.0, 0.0)]
    levels = []
    n = C
    while n >= 2:
        half = n // 2
        same = (row // n) == (col // n)
        row_2nd = ((row % n) >= half) != rev
        col_2nd = ((col % n) >= half) != rev
        mats.append(jnp.where(same & row_2nd & col_2nd & le, 1.0, 0.0)
                    - jnp.where(same & (~row_2nd) & (~col_2nd) & (~le), 1.0, 0.0))
        levels.append((same, ((trow % n) >= half) != rev))
        n = half
    mall = jnp.concatenate(mats, axis=0).astype(BF16)

    call = _mm_split3(mall, g_ref[...])
    b = call[:C]
    last = 0 if rev else C - 1
    b_last = b[last:last + 1, :]
    q = q_ref[...] * (DK_B ** -0.5)
    k = k_ref[...]
    qe = (q * jnp.exp(b)).astype(BF16)
    kd = (k * jnp.exp(b_last - b)).astype(BF16)
    dec = jnp.exp(b_last)
    qk = q * k
    qts, kts = [], []
    for l, (_, second) in enumerate(levels):
        e = jnp.exp(-jnp.abs(call[(l + 1) * C:(l + 2) * C]))
        qts.append(jnp.where(second, q * e, 0.0).astype(BF16))
        kts.append(jnp.where(second, 0.0, k * e).astype(BF16))
    vb = v_ref[...].astype(BF16)

    heads = range(H_B)
    ks = [slice(h * DK_B, (h + 1) * DK_B) for h in heads]
    vs = [slice(h * DV_B, (h + 1) * DV_B) for h in heads]
    St = [s_sc[h] for h in heads]
    o_inter = [_nt(qe[:, ks[h]], St[h].astype(BF16)) for h in heads]
    att = [jnp.where(row == col, jnp.sum(qk[:, ks[h]], axis=-1, keepdims=True), 0.0) for h in heads]
    for l, (same, _) in enumerate(levels):
        att = [att[h] + jnp.where(same, _nt(qts[l][:, ks[h]], kts[l][:, ks[h]]), 0.0) for h in heads]
    for h in heads:
        o_ref[:, vs[h]] = o_inter[h] + _mm(att[h].astype(BF16), vb[:, vs[h]])
        s_sc[h] = St[h] * dec[:, ks[h]] + _tn(vb[:, vs[h]], kd[:, ks[h]])


def gla_scan(p_all, g, n_lat, rev):
    L = p_all.shape[0]
    C = SCAN_CHUNK
    blk = functools.partial(_chunk_block, n_lat_chunks=n_lat // C, n_ctx_chunks=(L - n_lat) // C, rev=rev)
    qc, vc = 3 * D_A // D_B_K, (3 * D_A + 2 * D_B_K) // D_B_V
    return pl.pallas_call(
        functools.partial(_gla_kernel, rev=rev),
        grid=(L // C,),
        in_specs=[pl.BlockSpec((C, D_B_K), lambda c: (blk(c), qc)),
                  pl.BlockSpec((C, D_B_K), lambda c: (blk(c), qc + 1)),
                  pl.BlockSpec((C, D_B_V), lambda c: (blk(c), vc)),
                  pl.BlockSpec((C, D_B_K), lambda c: (blk(c), 0))],
        out_specs=pl.BlockSpec((C, D_B_V), lambda c: (blk(c), 0)),
        out_shape=jax.ShapeDtypeStruct((L, D_B_V), F32),
        scratch_shapes=[pltpu.VMEM((H_B, DV_B, DK_B), F32)],
        compiler_params=_cparams(("arbitrary",)),
    )(p_all, p_all, p_all, g)


def _seg(m, j):
    return m[0:2, j * D_MODEL:(j + 1) * D_MODEL]


def _coef(rows2):
    return jnp.zeros((SUBLANES, D_MODEL), F32).at[0:2].set(rows2)


def _ffn_half(h, m, kidx, gain, w13, w2, lead, n_lat, n_rows):
    shift, scale, gate = _seg(m, 3 * kidx), _seg(m, 3 * kidx + 1), _seg(m, 3 * kidx + 2)
    u = normmod(h, (gain[None, :] * (1.0 + scale))[:, None, :], shift[:, None, :], n_lat, n_rows, BF16)
    act = matmul_swiglu(u, w13, lead)
    return matmul_residual(act, w2, h, _coef(FFN_RES * gate), n_lat, lead)


def _shifted_lora(w, mu):
    return jnp.concatenate([(1.0 - mu[0] - mu[1])[:, None] * w, mu[0][:, None] * w, mu[1][:, None] * w], axis=1)


def _mixer_ab(u, P, n_lat):
    p_all = matmul(u, P['w_big'])
    r, k, v, kk, lw0, lw1, a0, a1, g, la0, la1 = ab_prep(p_all, P, n_lat)
    prep = dict(r=r, k=k, v=v, a=(a0, a1), g=g)
    yf = rwkv_scan(r, lw0, k, v, kk, a0, P['k_a'], n_lat, False)
    yb = rwkv_scan(r, lw1, k, v, kk, a1, P['k_a'], n_lat, True)
    of = gla_scan(p_all, la0, n_lat, False)
    ob = gla_scan(p_all, la1, n_lat, True)
    return ab_out(yf, yb, prep, of, ob, p_all, P, n_lat)


def _rope_tables(n_lat, n_rows):
    rows = n_lat // GRID_W
    rr = jnp.repeat(jnp.arange(rows), GRID_W)
    cc = jnp.tile(jnp.arange(GRID_W), rows)
    inv_freq = ROPE_THETA ** (-jnp.arange(ROPE_PAIRS, dtype=F32) / ROPE_PAIRS)
    ar = rr.astype(F32)[:, None] * inv_freq
    ac = cc.astype(F32)[:, None] * inv_freq
    cos = jnp.concatenate([jnp.cos(ar), jnp.cos(ar), jnp.cos(ac), jnp.cos(ac)], axis=1)
    sin = jnp.concatenate([-jnp.sin(ar), jnp.sin(ar), -jnp.sin(ac), jnp.sin(ac)], axis=1)
    n_ctx = n_rows - n_lat
    cos = jnp.concatenate([cos, jnp.ones((n_ctx, HEAD_C), F32)], axis=0)
    sin = jnp.concatenate([sin, jnp.zeros((n_ctx, HEAD_C), F32)], axis=0)
    return cos, sin


def _mixer_c(u, P, n_lat, need_ctx):
    L = u.shape[0]
    p = matmul(u, P['w_in'], P['lead'])
    q, k, v_ext = attn_prep(p, P['cos'], P['sin'], P['q_gain'], P['k_gain'])
    o_l = flash_gqa(q, k, v_ext, (0, n_lat), (0, L))
    if not need_ctx:
        return o_l
    o_c = flash_gqa(q, k, v_ext, (n_lat, L - n_lat), (n_lat, L - n_lat))
    return jnp.concatenate([o_l, o_c], axis=0)


def kernel(x, c, ctx, c_ctx, ada_down, ada_up, ada_bias, norm_gains, final_gain, ffn_w13, ffn_w2, ab_w_in, ab_w_out, rwkv_mu_rkv, rwkv_mu_lora, rwkv_w1, rwkv_w2, rwkv_w0, rwkv_a1, rwkv_a2, rwkv_a0, rwkv_g1, rwkv_g2, rwkv_k_k, rwkv_k_a, rwkv_r_k, rwkv_ln_w, rwkv_ln_b, gla_a1, gla_a2, gla_ab, gla_norm, attn_w_in, attn_w_out, attn_q_norm, attn_k_norm):
    D = D_MODEL
    n_lat = x.shape[1]
    h = jnp.concatenate([x[0], ctx[0]], axis=0)
    n_rows = h.shape[0]
    bf = lambda t: t.astype(BF16)
    cvec = jnp.zeros((16, D), F32).at[0].set(c[0]).at[1].set(c_ctx)
    cs = bf(cvec * jax.nn.sigmoid(cvec))
    cos, sin = _rope_tables(n_lat, n_rows)
    cat1 = lambda ts: jnp.concatenate(ts, axis=1)
    w13_s, w2_s, ada_down_s, ada_up_s = bf(ffn_w13), bf(ffn_w2), bf(ada_down), bf(ada_up)
    ab_w_out_s, attn_w_in_s, attn_w_out_s = bf(ab_w_out), bf(attn_w_in), bf(attn_w_out)
    for layer in range(DEPTH):
        last = layer == DEPTH - 1
        i = layer // 2
        m = matmul(bf(matmul(cs, ada_down_s, (layer,))), ada_up_s, (layer,)) + ada_bias[layer]
        gains = norm_gains[layer]
        h = _ffn_half(h, m, 0, gains[0], w13_s, w2_s, (layer, 0), n_lat, n_rows)
        u = normmod(h, (gains[1][None] * (1.0 + _seg(m, 4)))[:, None, :], _seg(m, 3)[:, None, :], n_lat, n_rows, BF16)
        if layer % 2 == 0:
            mu = rwkv_mu_lora[i]
            rg = R_GLA_GATE
            lora = cat1([_shifted_lora(cat1([rwkv_w1[i, 0], rwkv_w1[i, 1]]), mu[0]),
                         _shifted_lora(cat1([rwkv_a1[i, 0], rwkv_a1[i, 1]]), mu[1]),
                         _shifted_lora(rwkv_g1[i], mu[2]),
                         gla_a1[i, 0], gla_a1[i, 1]])
            lora = jnp.pad(lora, ((0, 0), (0, LORA_W - lora.shape[1])))
            ga2pad = jnp.stack([jnp.pad(gla_a2[i, d], ((d * rg, LANES - (d + 1) * rg), (0, 0))) for d in range(2)])
            P = dict(
                w_big=bf(cat1([ab_w_in[i], lora])), mu_rkv=rwkv_mu_rkv[i],
                w2=bf(rwkv_w2[i]), w0=rwkv_w0[i], a2=bf(rwkv_a2[i]), a0=rwkv_a0[i], g2=bf(rwkv_g2[i]),
                k_k=rwkv_k_k[i][None], k_a=rwkv_k_a[i][None], r_k=rwkv_r_k[i].reshape(1, D_A),
                ln_w=rwkv_ln_w[i][None], ln_b=rwkv_ln_b[i][None],
                ga2pad=bf(ga2pad), gab=gla_ab[i], g_norm=gla_norm[i][None])
            y = _mixer_ab(u, P, n_lat)
            w_out = ab_w_out_s
        else:
            P = dict(w_in=attn_w_in_s, lead=(i,), q_gain=attn_q_norm[i][None], k_gain=attn_k_norm[i][None],
                     cos=cos, sin=sin)
            y = _mixer_c(u, P, n_lat, not last)
            w_out = attn_w_out_s
        if last:
            n_rows = n_lat
        h = matmul_residual(y, w_out, h, _coef(_seg(m, 5)), n_lat, (i,))
        h = _ffn_half(h, m, 2, gains[2], w13_s, w2_s, (layer, 1), n_lat, n_rows)
    ones2 = jnp.stack([final_gain, final_gain])[:, None, :]
    out = normmod(h, ones2, jnp.zeros((2, 1, D), F32), n_lat, n_lat, F32)
    return out[None]
```

```python
import functools
import math

import jax
import jax.numpy as jnp
from jax import lax
from jax.experimental import pallas as pl
from jax.experimental.pallas import tpu as pltpu

F32 = jnp.float32
BF16 = jnp.bfloat16

D_MODEL = 4096
DEPTH = 4
GRID_W = 64
N_MOD = 9
D_FF = 6144
FFN_RES = 0.5
EPS = 1e-6
D_A = D_MODEL // 2
HEAD_A = 64
H_A = D_A // HEAD_A
R_DECAY = 128
R_AAA = 128
R_GATE = 256
RWKV_LN_EPS = 64e-5
D_B_K = D_MODEL // 4
D_B_V = D_MODEL // 2
H_B = 4
DK_B = D_B_K // H_B
DV_B = D_B_V // H_B
R_GLA_GATE = 16
GLA_GATE_NORM = 16.0
D_AB_IN = 3 * D_A + 2 * D_B_K + 2 * D_B_V
HEAD_C = 128
H_C = D_MODEL // HEAD_C
KV_C = 8
G_C = H_C // KV_C
KV_DIM = KV_C * HEAD_C
ROPE_PAIRS = HEAD_C // 4
ROPE_THETA = 10000.0
LOG2E = 1.4426950408889634

LANES = 128
SUBLANES = 8
SCAN_CHUNK = 64
VMEM_LIMIT = 56 * 1024 * 1024

LORA_W = 3072
AB_COLS = D_AB_IN + LORA_W
OFF_ZW, OFF_ZA, OFF_ZG = 0, 3 * 2 * R_DECAY, 3 * 2 * R_DECAY + 3 * 2 * R_AAA
OFF_ZGA = OFF_ZG + 3 * R_GATE


def _pick(n, prefs):
    for p in prefs:
        if n % p == 0:
            return p
    return n


def _cparams(sem):
    return pltpu.CompilerParams(dimension_semantics=sem, vmem_limit_bytes=VMEM_LIMIT)


def _row_tile(n_lat, n_rows, prefs):
    return _pick(math.gcd(n_lat, n_rows - n_lat) if n_rows > n_lat else n_lat, prefs)


def _sigmoid(x):
    return 1.0 / (1.0 + jnp.exp(-x))


def _softplus(x):
    return jnp.maximum(x, 0.0) + jnp.log(1.0 + jnp.exp(-jnp.abs(x)))


def _nt(a, b):
    return lax.dot_general(a, b, (((1,), (1,)), ((), ())), preferred_element_type=F32)


def _tn(a, b):
    return lax.dot_general(a, b, (((0,), (0,)), ((), ())), preferred_element_type=F32)


def _mm(a, b):
    return jnp.dot(a, b, preferred_element_type=F32)


def _mm_kernel(x_ref, w_ref, o_ref):
    o_ref[...] = _mm(x_ref[...], w_ref[...]).astype(o_ref.dtype)


def _mm_swiglu_kernel(x_ref, wg_ref, wu_ref, o_ref):
    x = x_ref[...]
    g = _mm(x, wg_ref[...])
    u = _mm(x, wu_ref[...])
    o_ref[...] = (g * _sigmoid(g) * u).astype(o_ref.dtype)


def _mm_res_kernel(x_ref, w_ref, h_ref, c_ref, o_ref, *, n_lat):
    tm = x_ref.shape[0]
    rows = pl.program_id(0) * tm + lax.broadcasted_iota(jnp.int32, (tm, 1), 0)
    coef = jnp.where(rows >= n_lat, c_ref[1:2, :], c_ref[0:1, :])
    o_ref[...] = h_ref[...] + coef * _mm(x_ref[...], w_ref[...])


_TM_PREFS = (1280, 1024, 640, 512, 256, 128, 64, 32, 16)


def _wspec(w, lead, tn, col):
    lead = tuple(lead)
    K = w.shape[-2]
    return pl.BlockSpec((None,) * len(lead) + (K, tn), lambda i, j: lead + (0, col(j)))


def matmul(x, w, lead=(), out_dtype=F32):
    M, K = x.shape
    N = w.shape[-1]
    tm = _pick(M, _TM_PREFS)
    wide = tm * K * 2 <= 8 * 1024 * 1024
    tn = _pick(N, (1024, 512, 256, 128) if wide else (512, 256, 128))
    return pl.pallas_call(
        _mm_kernel,
        grid=(M // tm, N // tn),
        in_specs=[pl.BlockSpec((tm, K), lambda i, j: (i, 0)),
                  _wspec(w, lead, tn, lambda j: j)],
        out_specs=pl.BlockSpec((tm, tn), lambda i, j: (i, j)),
        out_shape=jax.ShapeDtypeStruct((M, N), out_dtype),
        compiler_params=_cparams(("parallel", "parallel")),
    )(x, w)


def matmul_swiglu(x, w13, lead=()):
    M, K = x.shape
    F = w13.shape[-1] // 2
    tm = _pick(M, _TM_PREFS)
    tn = _pick(F, (512, 256, 128))
    nf = F // tn
    return pl.pallas_call(
        _mm_swiglu_kernel,
        grid=(M // tm, nf),
        in_specs=[pl.BlockSpec((tm, K), lambda i, j: (i, 0)),
                  _wspec(w13, lead, tn, lambda j: j),
                  _wspec(w13, lead, tn, lambda j: j + nf)],
        out_specs=pl.BlockSpec((tm, tn), lambda i, j: (i, j)),
        out_shape=jax.ShapeDtypeStruct((M, F), BF16),
        compiler_params=_cparams(("parallel", "parallel")),
    )(x, w13, w13)


def matmul_residual(x, w, h, coef, n_lat, lead=()):
    M, K = x.shape
    N = w.shape[-1]
    tm = _pick(M, _TM_PREFS)
    tn = _pick(N, (512, 256, 128))
    xmode = pl.Buffered(1) if K > D_MODEL else None
    return pl.pallas_call(
        functools.partial(_mm_res_kernel, n_lat=n_lat),
        grid=(M // tm, N // tn),
        in_specs=[pl.BlockSpec((tm, K), lambda i, j: (i, 0), pipeline_mode=xmode),
                  _wspec(w, lead, tn, lambda j: j),
                  pl.BlockSpec((tm, tn), lambda i, j: (i, j)),
                  pl.BlockSpec((SUBLANES, tn), lambda i, j: (0, j))],
        out_specs=pl.BlockSpec((tm, tn), lambda i, j: (i, j)),
        out_shape=jax.ShapeDtypeStruct((M, N), F32),
        compiler_params=_cparams(("parallel", "parallel")),
    )(x, w, h, coef)


def _normmod_kernel(h_ref, g_ref, s_ref, o_ref):
    x = h_ref[...]
    y = x * lax.rsqrt(jnp.mean(x * x, axis=-1, keepdims=True) + EPS)
    o_ref[...] = (y * g_ref[...] + s_ref[...]).astype(o_ref.dtype)


def normmod(h, g, s, n_lat, n_rows, out_dtype):
    D = h.shape[1]
    tm = _row_tile(n_lat, n_rows, (256, 128, 64, 32, 16, 8))
    seg = lambda i: (jnp.where(i * tm >= n_lat, 1, 0), 0, 0)
    return pl.pallas_call(
        _normmod_kernel,
        grid=(n_rows // tm,),
        in_specs=[pl.BlockSpec((tm, D), lambda i: (i, 0)),
                  pl.BlockSpec((None, 1, D), seg),
                  pl.BlockSpec((None, 1, D), seg)],
        out_specs=pl.BlockSpec((tm, D), lambda i: (i, 0)),
        out_shape=jax.ShapeDtypeStruct((n_rows, D), out_dtype),
        compiler_params=_cparams(("parallel",)),
    )(h, g, s)


def _flash_kernel(q_ref, k_ref, v_ref, o_ref, m_sc, acc_sc, *, tk):
    reps = tk // LANES
    m_sc[...] = jnp.full(m_sc.shape, -1e30, F32)
    acc_sc[...] = jnp.zeros(acc_sc.shape, F32)
    heads = range(G_C)
    tq = q_ref.shape[0]
    rh = min(tq, 512)
    units = [(g, r) for r in range(0, tq, rh) for g in heads]
    ahead = 2

    def kv_block(j, carry):
        off = pl.multiple_of(j * tk, tk)
        k = k_ref[pl.ds(off, tk), :]
        v = v_ref[pl.ds(off, tk), :]

        def scores(u):
            g, r = u
            return _nt(q_ref[r:r + rh, g * HEAD_C:(g + 1) * HEAD_C], k)

        def fold(u, s):
            g, r = u
            m_prev = m_sc[g, r:r + rh, :]
            m_new = jnp.maximum(m_prev, jnp.max(s, axis=-1, keepdims=True))
            p = jnp.exp2((s - jnp.concatenate([m_new] * reps, axis=1)).astype(BF16))
            alpha = jnp.exp2(m_prev - m_new)
            acc_sc[g, r:r + rh, :] = jnp.concatenate([alpha, alpha], axis=1) * acc_sc[g, r:r + rh, :] + _mm(p, v)
            m_sc[g, r:r + rh, :] = m_new

        pending = [scores(u) for u in units[:ahead]]
        for idx, u in enumerate(units):
            if idx + ahead < len(units):
                pending.append(scores(units[idx + ahead]))
            fold(u, pending[idx])
        return carry

    lax.fori_loop(0, k_ref.shape[0] // tk, kv_block, 0)
    for g in heads:
        a = acc_sc[g]
        o_ref[:, g * HEAD_C:(g + 1) * HEAD_C] = (a[:, :HEAD_C] / a[:, HEAD_C:]).astype(o_ref.dtype)


def flash_gqa(q, k, v_ext, q_rows, k_rows):
    q0, Lq = q_rows
    k0, Lk = k_rows
    assert k0 % Lk == 0
    tq = _pick(math.gcd(q0, Lq), (1024, 512, 256, 128))
    tk = _pick(Lk, (1280, 1024, 640, 512, 256, 128))
    qb, kb = q0 // tq, k0 // Lk
    gw = G_C * HEAD_C
    once = pl.Buffered(1)
    return pl.pallas_call(
        functools.partial(_flash_kernel, tk=tk),
        grid=(KV_C, Lq // tq),
        in_specs=[pl.BlockSpec((tq, gw), lambda n, i: (i + qb, n)),
                  pl.BlockSpec((Lk, HEAD_C), lambda n, i: (kb, n), pipeline_mode=once),
                  pl.BlockSpec((Lk, 2 * HEAD_C), lambda n, i: (kb, n), pipeline_mode=once)],
        out_specs=pl.BlockSpec((tq, gw), lambda n, i: (i, n)),
        out_shape=jax.ShapeDtypeStruct((Lq, H_C * HEAD_C), BF16),
        scratch_shapes=[pltpu.VMEM((G_C, tq, LANES), F32),
                        pltpu.VMEM((G_C, tq, 2 * HEAD_C), F32)],
        compiler_params=_cparams(("parallel", "arbitrary")),
    )(q, k, v_ext)


def _attn_prep_kernel(p_ref, cos_ref, sin_ref, qg_ref, kg_ref, q_ref, k_ref, v_ref):
    cos = cos_ref[...]
    sin = sin_ref[...]
    lane = lax.broadcasted_iota(jnp.int32, cos.shape, 1)
    first = (lane % (2 * ROPE_PAIRS)) < ROPE_PAIRS

    def norm_rope(x, gain):
        xn = x * lax.rsqrt(jnp.mean(x * x, axis=-1, keepdims=True) + EPS) * gain
        partner = jnp.where(first, pltpu.roll(xn, HEAD_C - ROPE_PAIRS, 1), pltpu.roll(xn, ROPE_PAIRS, 1))
        return xn * cos + partner * sin

    qg = qg_ref[...] * (LOG2E * HEAD_C ** -0.5)
    kg = kg_ref[...]
    for h in range(H_C):
        sl = slice(h * HEAD_C, (h + 1) * HEAD_C)
        q_ref[:, sl] = norm_rope(p_ref[:, sl], qg).astype(q_ref.dtype)
    ones = jnp.ones((p_ref.shape[0], HEAD_C), v_ref.dtype)
    for h in range(KV_C):
        sl = slice(h * HEAD_C, (h + 1) * HEAD_C)
        k_ref[:, sl] = norm_rope(p_ref[:, D_MODEL + h * HEAD_C:D_MODEL + (h + 1) * HEAD_C], kg).astype(k_ref.dtype)
        v_ref[:, 2 * h * HEAD_C:(2 * h + 1) * HEAD_C] = \
            p_ref[:, D_MODEL + KV_DIM + h * HEAD_C:D_MODEL + KV_DIM + (h + 1) * HEAD_C].astype(v_ref.dtype)
        v_ref[:, (2 * h + 1) * HEAD_C:(2 * h + 2) * HEAD_C] = ones


def attn_prep(p, cos, sin, q_gain, k_gain):
    L = p.shape[0]
    tm = _pick(L, (256, 128, 64, 32, 16, 8))
    row = lambda w: pl.BlockSpec((tm, w), lambda i: (i, 0))
    par = pl.BlockSpec((1, HEAD_C), lambda i: (0, 0))
    return pl.pallas_call(
        _attn_prep_kernel,
        grid=(L // tm,),
        in_specs=[row(D_MODEL + 2 * KV_DIM), row(HEAD_C), row(HEAD_C), par, par],
        out_specs=[row(D_MODEL), row(KV_DIM), row(2 * KV_DIM)],
        out_shape=[jax.ShapeDtypeStruct((L, D_MODEL), BF16),
                   jax.ShapeDtypeStruct((L, KV_DIM), BF16),
                   jax.ShapeDtypeStruct((L, 2 * KV_DIM), BF16)],
        compiler_params=_cparams(("parallel",)),
    )(p, cos, sin, q_gain, k_gain)


def _head_sum64(x):
    r_ = lax.broadcasted_iota(jnp.int32, (LANES, LANES), 0) // HEAD_A
    c_ = lax.broadcasted_iota(jnp.int32, (LANES, LANES), 1) // HEAD_A
    ones_bd = (r_ == c_).astype(BF16)
    hi = x.astype(BF16)
    lo = (x - hi.astype(F32)).astype(BF16)
    tiles = [_mm(hi[:, j:j + LANES], ones_bd) + _mm(lo[:, j:j + LANES], ones_bd) for j in range(0, x.shape[1], LANES)]
    return jnp.concatenate(tiles, axis=1)


def _ab_prep_kernel(x_ref, xp_ref, xn_ref, z_ref, zp_ref, zn_ref, mu_ref, w2_ref, w0_ref, a2_ref, a0_ref,
                    g2_ref, kk_ref, ga2_ref, gab_ref,
                    r_o, k_o, v_o, kkn_o, lw0_o, lw1_o, a0_o, a1_o, g_o, la0_o, la1_o, *, n_lat, n_rows):
    tm = x_ref.shape[0]
    i = pl.program_id(0)
    loc = lax.broadcasted_iota(jnp.int32, (tm, 1), 0)
    rows = i * tm + loc
    seg_first = (rows == 0) | (rows == n_lat)
    seg_last = (rows == n_lat - 1) | (rows == n_rows - 1)

    def prev(x, halo):
        y = jnp.where(loc == 0, halo[SUBLANES - 1:SUBLANES, :], pltpu.roll(x, 1, 0))
        return jnp.where(seg_first, 0.0, y)

    def nxt(x, halo):
        y = jnp.where(loc == tm - 1, halo[0:1, :], pltpu.roll(x, tm - 1, 0))
        return jnp.where(seg_last, 0.0, y)

    x = x_ref[...]
    mu = mu_ref[...]
    rkv = x + mu[0:1, :] * (prev(x, xp_ref[...]) - x) + mu[1:2, :] * (nxt(x, xn_ref[...]) - x)
    r_o[...] = rkv[:, :D_A]
    k = rkv[:, D_A:2 * D_A]
    k_o[...] = k
    v_o[...] = rkv[:, 2 * D_A:]
    kk = k * kk_ref[...]
    kkn_o[...] = kk / jnp.maximum(jnp.sqrt(_head_sum64(kk * kk)), 1e-12)

    def lora(off, width):
        c = z_ref[:, off:off + width]
        p_ = prev(z_ref[:, off + width:off + 2 * width], zp_ref[:, off + width:off + 2 * width])
        n_ = nxt(z_ref[:, off + 2 * width:off + 3 * width], zn_ref[:, off + 2 * width:off + 3 * width])
        return c + p_ + n_

    zw = jnp.tanh(lora(OFF_ZW, 2 * R_DECAY)).astype(BF16)
    za = lora(OFF_ZA, 2 * R_AAA).astype(BF16)
    zg = _sigmoid(lora(OFF_ZG, R_GATE)).astype(BF16)
    zga = z_ref[:, OFF_ZGA:OFF_ZGA + LANES].astype(BF16)
    g_o[...] = _mm(zg, g2_ref[...])
    for d, (lw_o, a_o, la_o) in enumerate(((lw0_o, a0_o, la0_o), (lw1_o, a1_o, la1_o))):
        lw = _mm(zw[:, d * R_DECAY:(d + 1) * R_DECAY], w2_ref[d]) + w0_ref[d:d + 1, :]
        lw_o[...] = -jnp.exp(-_softplus(-lw) - 0.5)
        a_o[...] = _sigmoid(_mm(za[:, d * R_AAA:(d + 1) * R_AAA], a2_ref[d]) + a0_ref[d:d + 1, :])
        zz = _mm(zga, ga2_ref[d]) + gab_ref[d:d + 1, :]
        la_o[...] = -_softplus(-zz) * (1.0 / GLA_GATE_NORM)


def ab_prep(p_all, P, n_lat):
    L = p_all.shape[0]
    tm = _row_tile(n_lat, L, (128, 64, 32, 16, 8))
    nb8 = tm // SUBLANES
    last8 = L // SUBLANES - 1
    wx, cz = 3 * D_A, D_AB_IN // LORA_W
    full = lambda shape: pl.BlockSpec(shape, lambda i: (0,) * len(shape))
    outw = lambda w: pl.BlockSpec((tm, w), lambda i: (i, 0))
    in_specs = [
        pl.BlockSpec((tm, wx), lambda i: (i, 0)),
        pl.BlockSpec((SUBLANES, wx), lambda i: (jnp.maximum(i * nb8 - 1, 0), 0)),
        pl.BlockSpec((SUBLANES, wx), lambda i: (jnp.minimum((i + 1) * nb8, last8), 0)),
        pl.BlockSpec((tm, LORA_W), lambda i: (i, cz)),
        pl.BlockSpec((SUBLANES, LORA_W), lambda i: (jnp.maximum(i * nb8 - 1, 0), cz)),
        pl.BlockSpec((SUBLANES, LORA_W), lambda i: (jnp.minimum((i + 1) * nb8, last8), cz)),
        full((2, wx)), full((2, R_DECAY, D_A)), full((2, D_A)), full((2, R_AAA, D_A)), full((2, D_A)),
        full((R_GATE, D_A)), full((1, D_A)), full((2, LANES, D_B_K)), full((2, D_B_K)),
    ]
    outs = [D_A] * 9 + [D_B_K] * 2
    return pl.pallas_call(
        functools.partial(_ab_prep_kernel, n_lat=n_lat, n_rows=L),
        grid=(L // tm,),
        in_specs=in_specs,
        out_specs=[outw(w) for w in outs],
        out_shape=[jax.ShapeDtypeStruct((L, w), F32) for w in outs],
        compiler_params=_cparams(("parallel",)),
    )(p_all, p_all, p_all, p_all, p_all, p_all, P['mu_rkv'], P['w2'], P['w0'], P['a2'], P['a0'],
      P['g2'], P['k_k'], P['ga2pad'], P['gab'])


def _ab_out_kernel(yf_ref, yb_ref, r_ref, k_ref, v_ref, a0_ref, a1_ref, g_ref, of_ref, ob_ref, go_ref,
                   rk_ref, ka_ref, lnw_ref, lnb_ref, gn_ref, o_ref):
    y = yf_ref[...] + yb_ref[...]
    mean = _head_sum64(y) * (1.0 / HEAD_A)
    d = y - mean
    var = _head_sum64(d * d) * (1.0 / HEAD_A)
    yn = d * lax.rsqrt(var + RWKV_LN_EPS) * lnw_ref[...] + lnb_ref[...]
    kdsum = k_ref[...] * (2.0 + (a0_ref[...] + a1_ref[...] - 2.0) * ka_ref[...])
    bonus = _head_sum64(r_ref[...] * kdsum * rk_ref[...]) * v_ref[...]
    o_ref[:, :D_A] = ((yn + bonus) * g_ref[...]).astype(o_ref.dtype)
    o = of_ref[...] + ob_ref[...]
    go = go_ref[...]
    for h in range(H_B):
        sl = slice(h * DV_B, (h + 1) * DV_B)
        oh = o[:, sl]
        yh = oh * lax.rsqrt(jnp.mean(oh * oh, axis=-1, keepdims=True) + EPS) * gn_ref[...]
        gh = go[:, sl]
        o_ref[:, D_A + h * DV_B:D_A + (h + 1) * DV_B] = (yh * gh * _sigmoid(gh)).astype(o_ref.dtype)


def ab_out(yf, yb, prep, of, ob, p_all, P, n_lat):
    L = yf.shape[0]
    tm = _row_tile(n_lat, L, (128, 64, 32, 16, 8))
    rw = pl.BlockSpec((tm, D_A), lambda i: (i, 0))
    par = lambda w: pl.BlockSpec((1, w), lambda i: (0, 0))
    go_spec = pl.BlockSpec((tm, D_B_V), lambda i: (i, (3 * D_A + 2 * D_B_K + D_B_V) // D_B_V))
    return pl.pallas_call(
        _ab_out_kernel,
        grid=(L // tm,),
        in_specs=[rw] * 10 + [go_spec, par(D_A), par(D_A), par(D_A), par(D_A), par(DV_B)],
        out_specs=pl.BlockSpec((tm, D_MODEL), lambda i: (i, 0)),
        out_shape=jax.ShapeDtypeStruct((L, D_MODEL), BF16),
        compiler_params=_cparams(("parallel",)),
    )(yf, yb, prep['r'], prep['k'], prep['v'], prep['a'][0], prep['a'][1], prep['g'], of, ob, p_all,
      P['r_k'], P['k_a'], P['ln_w'], P['ln_b'], P['g_norm'])


def _mm_split3(m, x):
    hi = x.astype(BF16)
    r1 = x - hi.astype(F32)
    mid = r1.astype(BF16)
    lo = (r1 - mid.astype(F32)).astype(BF16)
    return _mm(m, hi) + _mm(m, mid) + _mm(m, lo)


def _chunk_block(c, n_lat_chunks, n_ctx_chunks, rev):
    if rev:
        return n_lat_chunks + n_ctx_chunks - 1 - c
    return jnp.where(c < n_ctx_chunks, n_lat_chunks + c, c - n_ctx_chunks)


def _rwkv_kernel(r_ref, lw_ref, k_ref, v_ref, kk_ref, a_ref, ka_ref, y_ref, s_sc, *, npairs, rev):
    C = SCAN_CHUNK
    C2 = 2 * C

    @pl.when(pl.program_id(1) == 0)
    def _():
        s_sc[...] = jnp.zeros(s_sc.shape, F32)

    row = lax.broadcasted_iota(jnp.int32, (C, C), 0)
    col = lax.broadcasted_iota(jnp.int32, (C, C), 1)
    tri = ((col >= row) if rev else (col <= row)).astype(BF16)
    row2 = lax.broadcasted_iota(jnp.int32, (C2, C2), 0)
    col2 = lax.broadcasted_iota(jnp.int32, (C2, C2), 1)
    same = (row2 // C) == (col2 // C)
    strict2 = same & ((col2 > row2) if rev else (col2 < row2))
    incl2 = same & ((col2 >= row2) if rev else (col2 <= row2))
    eye2 = (row2 == col2).astype(F32)
    last = 0 if rev else C - 1
    head0 = lax.broadcasted_iota(jnp.int32, (C, LANES), 1) < HEAD_A

    lw = lw_ref[...]
    cl = _mm_split3(tri, lw)
    cl_last = cl[last:last + 1, :]
    kk = kk_ref[...]
    a = a_ref[...]
    k = k_ref[...] * (1.0 + (a - 1.0) * ka_ref[...])
    b = a * kk
    p_inv = jnp.exp(-cl)
    tail = jnp.exp(cl_last - cl)
    kkd = kk * jnp.exp(cl - lw)
    rd = r_ref[...] * jnp.exp(cl)
    kinv = k * p_inv
    binv = b * p_inv
    kdec = k * tail
    nbdec = -(b * tail)
    v = v_ref[...]
    decay_last = jnp.exp(cl_last)

    def stack(x, p):
        xs = x[:, p * LANES:(p + 1) * LANES]
        return jnp.concatenate([jnp.where(head0, xs, 0.0), jnp.where(head0, 0.0, xs)], axis=0).astype(BF16)

    pairs = range(npairs)
    cat0 = lambda a_, b_: jnp.concatenate([a_, b_], axis=0)
    lhs = [cat0(stack(kkd, p), stack(rd, p)) for p in pairs]
    rhs_g = [cat0(stack(kinv, p), stack(binv, p)) for p in pairs]
    dec2 = [cat0(stack(kdec, p), stack(nbdec, p)) for p in pairs]
    v2 = [stack(v, p) for p in pairs]
    G = [_nt(lhs[p], rhs_g[p]) for p in pairs]
    a_kk = [jnp.where(strict2, G[p][:C2, :C2], 0.0).astype(BF16) for p in pairs]
    n1 = [jnp.where(strict2, -G[p][:C2, C2:], 0.0).astype(BF16) for p in pairs]
    b_rkb = [jnp.concatenate([jnp.where(incl2, G[p][C2:, :C2], 0.0),
                              jnp.where(incl2, -G[p][C2:, C2:], 0.0)], axis=1).astype(BF16) for p in pairs]

    powers = [n1]
    for _ in range(5):
        prev = powers[-1]
        powers.append([_mm(prev[p], prev[p]).astype(BF16) for p in pairs])
    tinv = [eye2 + n1[p].astype(F32) for p in pairs]
    for npow in powers[1:]:
        tinv = [tinv[p] + _mm(npow[p], tinv[p].astype(BF16)) for p in pairs]
    tinv = [t.astype(BF16) for t in tinv]

    S = [s_sc[p] for p in pairs]
    inter = [_nt(lhs[p], S[p].astype(BF16)) for p in pairs]
    rhs = [inter[p][:C2] + _mm(a_kk[p], v2[p]) for p in pairs]
    uf = [_mm(tinv[p], rhs[p].astype(BF16)) for p in pairs]
    res = [rhs[p] - uf[p] + _mm(n1[p], uf[p].astype(BF16)) for p in pairs]
    u2 = [(uf[p] + _mm(tinv[p], res[p].astype(BF16))).astype(BF16) for p in pairs]
    vu = [cat0(v2[p], u2[p]) for p in pairs]
    y2 = [inter[p][C2:] + _mm(b_rkb[p], vu[p]) for p in pairs]
    for p in pairs:
        sl = slice(p * LANES, (p + 1) * LANES)
        y_ref[:, sl] = y2[p][:C] + y2[p][C:]
        s_sc[p] = S[p] * decay_last[:, sl] + _tn(vu[p], dec2[p])


def rwkv_scan(r, lw, k, v, kk, a, k_a, n_lat, rev):
    L = r.shape[0]
    C = SCAN_CHUNK
    gw = D_A
    npairs = gw // LANES
    blk = functools.partial(_chunk_block, n_lat_chunks=n_lat // C, n_ctx_chunks=(L - n_lat) // C, rev=rev)
    spec = pl.BlockSpec((C, gw), lambda g, c: (blk(c), g))
    return pl.pallas_call(
        functools.partial(_rwkv_kernel, npairs=npairs, rev=rev),
        grid=(D_A // gw, L // C),
        in_specs=[spec] * 6 + [pl.BlockSpec((1, gw), lambda g, c: (0, g))],
        out_specs=spec,
        out_shape=jax.ShapeDtypeStruct((L, D_A), F32),
        scratch_shapes=[pltpu.VMEM((npairs, LANES, LANES), F32)],
        compiler_params=_cparams(("parallel", "arbitrary")),
    )(r, lw, k, v, kk, a, k_a)


def _gla_kernel(q_ref, k_ref, v_ref, g_ref, o_ref, s_sc, *, rev):
    C = SCAN_CHUNK

    @pl.when(pl.program_id(0) == 0)
    def _():
        s_sc[...] = jnp.zeros(s_sc.shape, F32)

    row = lax.broadcasted_iota(jnp.int32, (C, C), 0)
    col = lax.broadcasted_iota(jnp.int32, (C, C), 1)
    le = (col >= row) if rev else (col <= row)
    trow = lax.broadcasted_iota(jnp.int32, (C, 1), 0)
    mats = [jnp.where(le, 1.0, 0.0)]
    levels = []
    n = C
    while n >= 2:
        half = n // 2
        same = (row // n) == (col // n)
        row_2nd = ((row % n) >= half) != rev
        col_2nd = ((col % n) >= half) != rev
        mats.append(jnp.where(same & row_2nd & col_2nd & le, 1.0, 0.0)
                    - jnp.where(same & (~row_2nd) & (~col_2nd) & (~le), 1.0, 0.0))
        levels.append((same, ((trow % n) >= half) != rev))
        n = half
    mall = jnp.concatenate(mats, axis=0).astype(BF16)

    call = _mm_split3(mall, g_ref[...])
    b = call[:C]
    last = 0 if rev else C - 1
    b_last = b[last:last + 1, :]
    q = q_ref[...] * (DK_B ** -0.5)
    k = k_ref[...]
    qe = (q * jnp.exp(b)).astype(BF16)
    kd = (k * jnp.exp(b_last - b)).astype(BF16)
    dec = jnp.exp(b_last)
    qk = q * k
    qts, kts = [], []
    for l, (_, second) in enumerate(levels):
        e = jnp.exp(-jnp.abs(call[(l + 1) * C:(l + 2) * C]))
        qts.append(jnp.where(second, q * e, 0.0).astype(BF16))
        kts.append(jnp.where(second, 0.0, k * e).astype(BF16))
    vb = v_ref[...].astype(BF16)

    heads = range(H_B)
    ks = [slice(h * DK_B, (h + 1) * DK_B) for h in heads]
    vs = [slice(h * DV_B, (h + 1) * DV_B) for h in heads]
    St = [s_sc[h] for h in heads]
    o_inter = [_nt(qe[:, ks[h]], St[h].astype(BF16)) for h in heads]
    att = [jnp.where(row == col, jnp.sum(qk[:, ks[h]], axis=-1, keepdims=True), 0.0) for h in heads]
    for l, (same, _) in enumerate(levels):
        att = [att[h] + jnp.where(same, _nt(qts[l][:, ks[h]], kts[l][:, ks[h]]), 0.0) for h in heads]
    for h in heads:
        o_ref[:, vs[h]] = o_inter[h] + _mm(att[h].astype(BF16), vb[:, vs[h]])
        s_sc[h] = St[h] * dec[:, ks[h]] + _tn(vb[:, vs[h]], kd[:, ks[h]])


def gla_scan(p_all, g, n_lat, rev):
    L = p_all.shape[0]
    C = SCAN_CHUNK
    blk = functools.partial(_chunk_block, n_lat_chunks=n_lat // C, n_ctx_chunks=(L - n_lat) // C, rev=rev)
    qc, vc = 3 * D_A // D_B_K, (3 * D_A + 2 * D_B_K) // D_B_V
    return pl.pallas_call(
        functools.partial(_gla_kernel, rev=rev),
        grid=(L // C,),
        in_specs=[pl.BlockSpec((C, D_B_K), lambda c: (blk(c), qc)),
                  pl.BlockSpec((C, D_B_K), lambda c: (blk(c), qc + 1)),
                  pl.BlockSpec((C, D_B_V), lambda c: (blk(c), vc)),
                  pl.BlockSpec((C, D_B_K), lambda c: (blk(c), 0))],
        out_specs=pl.BlockSpec((C, D_B_V), lambda c: (blk(c), 0)),
        out_shape=jax.ShapeDtypeStruct((L, D_B_V), F32),
        scratch_shapes=[pltpu.VMEM((H_B, DV_B, DK_B), F32)],
        compiler_params=_cparams(("arbitrary",)),
    )(p_all, p_all, p_all, g)


def _seg(m, j):
    return m[0:2, j * D_MODEL:(j + 1) * D_MODEL]


def _coef(rows2):
    return jnp.zeros((SUBLANES, D_MODEL), F32).at[0:2].set(rows2)


def _ffn_half(h, m, kidx, gain, w13, w2, lead, n_lat, n_rows):
    shift, scale, gate = _seg(m, 3 * kidx), _seg(m, 3 * kidx + 1), _seg(m, 3 * kidx + 2)
    u = normmod(h, (gain[None, :] * (1.0 + scale))[:, None, :], shift[:, None, :], n_lat, n_rows, BF16)
    act = matmul_swiglu(u, w13, lead)
    return matmul_residual(act, w2, h, _coef(FFN_RES * gate), n_lat, lead)


def _shifted_lora(w, mu):
    return jnp.concatenate([(1.0 - mu[0] - mu[1])[:, None] * w, mu[0][:, None] * w, mu[1][:, None] * w], axis=1)


def _mixer_ab(u, P, n_lat):
    p_all = matmul(u, P['w_big'])
    r, k, v, kk, lw0, lw1, a0, a1, g, la0, la1 = ab_prep(p_all, P, n_lat)
    prep = dict(r=r, k=k, v=v, a=(a0, a1), g=g)
    yf = rwkv_scan(r, lw0, k, v, kk, a0, P['k_a'], n_lat, False)
    yb = rwkv_scan(r, lw1, k, v, kk, a1, P['k_a'], n_lat, True)
    of = gla_scan(p_all, la0, n_lat, False)
    ob = gla_scan(p_all, la1, n_lat, True)
    return ab_out(yf, yb, prep, of, ob, p_all, P, n_lat)


def _rope_tables(n_lat, n_rows):
    rows = n_lat // GRID_W
    rr = jnp.repeat(jnp.arange(rows), GRID_W)
    cc = jnp.tile(jnp.arange(GRID_W), rows)
    inv_freq = ROPE_THETA ** (-jnp.arange(ROPE_PAIRS, dtype=F32) / ROPE_PAIRS)
    ar = rr.astype(F32)[:, None] * inv_freq
    ac = cc.astype(F32)[:, None] * inv_freq
    cos = jnp.concatenate([jnp.cos(ar), jnp.cos(ar), jnp.cos(ac), jnp.cos(ac)], axis=1)
    sin = jnp.concatenate([-jnp.sin(ar), jnp.sin(ar), -jnp.sin(ac), jnp.sin(ac)], axis=1)
    n_ctx = n_rows - n_lat
    cos = jnp.concatenate([cos, jnp.ones((n_ctx, HEAD_C), F32)], axis=0)
    sin = jnp.concatenate([sin, jnp.zeros((n_ctx, HEAD_C), F32)], axis=0)
    return cos, sin


def _mixer_c(u, P, n_lat, need_ctx):
    L = u.shape[0]
    p = matmul(u, P['w_in'], P['lead'])
    q, k, v_ext = attn_prep(p, P['cos'], P['sin'], P['q_gain'], P['k_gain'])
    o_l = flash_gqa(q, k, v_ext, (0, n_lat), (0, L))
    if not need_ctx:
        return o_l
    o_c = flash_gqa(q, k, v_ext, (n_lat, L - n_lat), (n_lat, L - n_lat))
    return jnp.concatenate([o_l, o_c], axis=0)


def kernel(x, c, ctx, c_ctx, ada_down, ada_up, ada_bias, norm_gains, final_gain, ffn_w13, ffn_w2, ab_w_in, ab_w_out, rwkv_mu_rkv, rwkv_mu_lora, rwkv_w1, rwkv_w2, rwkv_w0, rwkv_a1, rwkv_a2, rwkv_a0, rwkv_g1, rwkv_g2, rwkv_k_k, rwkv_k_a, rwkv_r_k, rwkv_ln_w, rwkv_ln_b, gla_a1, gla_a2, gla_ab, gla_norm, attn_w_in, attn_w_out, attn_q_norm, attn_k_norm):
    D = D_MODEL
    n_lat = x.shape[1]
    h = jnp.concatenate([x[0], ctx[0]], axis=0)
    n_rows = h.shape[0]
    bf = lambda t: t.astype(BF16)
    cvec = jnp.zeros((16, D), F32).at[0].set(c[0]).at[1].set(c_ctx)
    cs = bf(cvec * jax.nn.sigmoid(cvec))
    cos, sin = _rope_tables(n_lat, n_rows)
    cat1 = lambda ts: jnp.concatenate(ts, axis=1)
    w13_s, w2_s, ada_down_s, ada_up_s = bf(ffn_w13), bf(ffn_w2), bf(ada_down), bf(ada_up)
    ab_w_out_s, attn_w_in_s, attn_w_out_s = bf(ab_w_out), bf(attn_w_in), bf(attn_w_out)
    for layer in range(DEPTH):
        last = layer == DEPTH - 1
        i = layer // 2
        m = matmul(bf(matmul(cs, ada_down_s, (layer,))), ada_up_s, (layer,)) + ada_bias[layer]
        gains = norm_gains[layer]
        h = _ffn_half(h, m, 0, gains[0], w13_s, w2_s, (layer, 0), n_lat, n_rows)
        u = normmod(h, (gains[1][None] * (1.0 + _seg(m, 4)))[:, None, :], _seg(m, 3)[:, None, :], n_lat, n_rows, BF16)
        if layer % 2 == 0:
            mu = rwkv_mu_lora[i]
            rg = R_GLA_GATE
            lora = cat1([_shifted_lora(cat1([rwkv_w1[i, 0], rwkv_w1[i, 1]]), mu[0]),
                         _shifted_lora(cat1([rwkv_a1[i, 0], rwkv_a1[i, 1]]), mu[1]),
                         _shifted_lora(rwkv_g1[i], mu[2]),
                         gla_a1[i, 0], gla_a1[i, 1]])
            lora = jnp.pad(lora, ((0, 0), (0, LORA_W - lora.shape[1])))
            ga2pad = jnp.stack([jnp.pad(gla_a2[i, d], ((d * rg, LANES - (d + 1) * rg), (0, 0))) for d in range(2)])
            P = dict(
                w_big=bf(cat1([ab_w_in[i], lora])), mu_rkv=rwkv_mu_rkv[i],
                w2=bf(rwkv_w2[i]), w0=rwkv_w0[i], a2=bf(rwkv_a2[i]), a0=rwkv_a0[i], g2=bf(rwkv_g2[i]),
                k_k=rwkv_k_k[i][None], k_a=rwkv_k_a[i][None], r_k=rwkv_r_k[i].reshape(1, D_A),
                ln_w=rwkv_ln_w[i][None], ln_b=rwkv_ln_b[i][None],
                ga2pad=bf(ga2pad), gab=gla_ab[i], g_norm=gla_norm[i][None])
            y = _mixer_ab(u, P, n_lat)
            w_out = ab_w_out_s
        else:
            P = dict(w_in=attn_w_in_s, lead=(i,), q_gain=attn_q_norm[i][None], k_gain=attn_k_norm[i][None],
                     cos=cos, sin=sin)
            y = _mixer_c(u, P, n_lat, not last)
            w_out = attn_w_out_s
        if last:
            n_rows = n_lat
        h = matmul_residual(y, w_out, h, _coef(_seg(m, 5)), n_lat, (i,))
        h = _ffn_half(h, m, 2, gains[2], w13_s, w2_s, (layer, 1), n_lat, n_rows)
    ones2 = jnp.stack([final_gain, final_gain])[:, None, :]
    out = normmod(h, ones2, jnp.zeros((2, 1, D), F32), n_lat, n_lat, F32)
    return out[None]
```

```python
import functools
import math

import jax
import jax.numpy as jnp
from jax import lax
from jax.experimental import pallas as pl
from jax.experimental.pallas import tpu as pltpu

F32 = jnp.float32
BF16 = jnp.bfloat16

D_MODEL = 4096
DEPTH = 4
GRID_W = 64
N_MOD = 9
D_FF = 6144
FFN_RES = 0.5
EPS = 1e-6
D_A = D_MODEL // 2
HEAD_A = 64
H_A = D_A // HEAD_A
R_DECAY = 128
R_AAA = 128
R_GATE = 256
RWKV_LN_EPS = 64e-5
D_B_K = D_MODEL // 4
D_B_V = D_MODEL // 2
H_B = 4
DK_B = D_B_K // H_B
DV_B = D_B_V // H_B
R_GLA_GATE = 16
GLA_GATE_NORM = 16.0
D_AB_IN = 3 * D_A + 2 * D_B_K + 2 * D_B_V
HEAD_C = 128
H_C = D_MODEL // HEAD_C
KV_C = 8
G_C = H_C // KV_C
KV_DIM = KV_C * HEAD_C
ROPE_PAIRS = HEAD_C // 4
ROPE_THETA = 10000.0
LOG2E = 1.4426950408889634

LANES = 128
SUBLANES = 8
SCAN_CHUNK = 64
VMEM_LIMIT = 60 * 1024 * 1024

LORA_W = 3072
AB_COLS = D_AB_IN + LORA_W
OFF_ZW, OFF_ZA, OFF_ZG = 0, 3 * 2 * R_DECAY, 3 * 2 * R_DECAY + 3 * 2 * R_AAA
OFF_ZGA = OFF_ZG + 3 * R_GATE


def _pick(n, prefs):
    for p in prefs:
        if n % p == 0:
            return p
    return n


def _cparams(sem):
    return pltpu.CompilerParams(dimension_semantics=sem, vmem_limit_bytes=VMEM_LIMIT)


def _row_tile(n_lat, n_rows, prefs):
    return _pick(math.gcd(n_lat, n_rows - n_lat) if n_rows > n_lat else n_lat, prefs)


def _sigmoid(x):
    return 1.0 / (1.0 + jnp.exp(-x))


def _softplus(x):
    return jnp.maximum(x, 0.0) + jnp.log(1.0 + jnp.exp(-jnp.abs(x)))


def _nt(a, b):
    return lax.dot_general(a, b, (((1,), (1,)), ((), ())), preferred_element_type=F32)


def _tn(a, b):
    return lax.dot_general(a, b, (((0,), (0,)), ((), ())), preferred_element_type=F32)


def _mm(a, b):
    return jnp.dot(a, b, preferred_element_type=F32)


def _mm_kernel(x_ref, w_ref, o_ref):
    o_ref[...] = _mm(x_ref[...], w_ref[...]).astype(o_ref.dtype)


def _mm_swiglu_kernel(x_ref, wg_ref, wu_ref, o_ref):
    x = x_ref[...]
    g = _mm(x, wg_ref[...])
    u = _mm(x, wu_ref[...])
    o_ref[...] = (g * _sigmoid(g) * u).astype(o_ref.dtype)


def _mm_res_kernel(x_ref, w_ref, h_ref, c_ref, o_ref, *, n_lat):
    tm = x_ref.shape[0]
    rows = pl.program_id(0) * tm + lax.broadcasted_iota(jnp.int32, (tm, 1), 0)
    coef = jnp.where(rows >= n_lat, c_ref[1:2, :], c_ref[0:1, :])
    o_ref[...] = h_ref[...] + coef * _mm(x_ref[...], w_ref[...])


_TM_PREFS = (1280, 1024, 640, 512, 256, 128, 64, 32, 16)


def _wspec(w, lead, tn, col):
    lead = tuple(lead)
    K = w.shape[-2]
    return pl.BlockSpec((None,) * len(lead) + (K, tn), lambda i, j: lead + (0, col(j)))


def matmul(x, w, lead=(), out_dtype=F32):
    M, K = x.shape
    N = w.shape[-1]
    tm = _pick(M, _TM_PREFS)
    wide = tm * K * 2 <= 8 * 1024 * 1024
    tn = _pick(N, (1024, 512, 256, 128) if wide else (512, 256, 128))
    return pl.pallas_call(
        _mm_kernel,
        grid=(M // tm, N // tn),
        in_specs=[pl.BlockSpec((tm, K), lambda i, j: (i, 0)),
                  _wspec(w, lead, tn, lambda j: j)],
        out_specs=pl.BlockSpec((tm, tn), lambda i, j: (i, j)),
        out_shape=jax.ShapeDtypeStruct((M, N), out_dtype),
        compiler_params=_cparams(("parallel", "parallel")),
    )(x, w)


def matmul_swiglu(x, w13, lead=()):
    M, K = x.shape
    F = w13.shape[-1] // 2
    tm = _pick(M, _TM_PREFS)
    tn = _pick(F, (512, 256, 128))
    nf = F // tn
    return pl.pallas_call(
        _mm_swiglu_kernel,
        grid=(M // tm, nf),
        in_specs=[pl.BlockSpec((tm, K), lambda i, j: (i, 0)),
                  _wspec(w13, lead, tn, lambda j: j),
                  _wspec(w13, lead, tn, lambda j: j + nf)],
        out_specs=pl.BlockSpec((tm, tn), lambda i, j: (i, j)),
        out_shape=jax.ShapeDtypeStruct((M, F), BF16),
        compiler_params=_cparams(("parallel", "parallel")),
    )(x, w13, w13)


def matmul_residual(x, w, h, coef, n_lat, lead=()):
    M, K = x.shape
    N = w.shape[-1]
    tm = _pick(M, _TM_PREFS)
    tn = _pick(N, (512, 256, 128))
    xmode = None
    return pl.pallas_call(
        functools.partial(_mm_res_kernel, n_lat=n_lat),
        grid=(M // tm, N // tn),
        in_specs=[pl.BlockSpec((tm, K), lambda i, j: (i, 0), pipeline_mode=xmode),
                  _wspec(w, lead, tn, lambda j: j),
                  pl.BlockSpec((tm, tn), lambda i, j: (i, j)),
                  pl.BlockSpec((SUBLANES, tn), lambda i, j: (0, j))],
        out_specs=pl.BlockSpec((tm, tn), lambda i, j: (i, j)),
        out_shape=jax.ShapeDtypeStruct((M, N), F32),
        compiler_params=_cparams(("parallel", "parallel")),
    )(x, w, h, coef)


def _normmod_kernel(h_ref, g_ref, s_ref, o_ref):
    x = h_ref[...]
    y = x * lax.rsqrt(jnp.mean(x * x, axis=-1, keepdims=True) + EPS)
    o_ref[...] = (y * g_ref[...] + s_ref[...]).astype(o_ref.dtype)


def normmod(h, g, s, n_lat, n_rows, out_dtype):
    D = h.shape[1]
    tm = _row_tile(n_lat, n_rows, (256, 128, 64, 32, 16, 8))
    seg = lambda i: (jnp.where(i * tm >= n_lat, 1, 0), 0, 0)
    return pl.pallas_call(
        _normmod_kernel,
        grid=(n_rows // tm,),
        in_specs=[pl.BlockSpec((tm, D), lambda i: (i, 0)),
                  pl.BlockSpec((None, 1, D), seg),
                  pl.BlockSpec((None, 1, D), seg)],
        out_specs=pl.BlockSpec((tm, D), lambda i: (i, 0)),
        out_shape=jax.ShapeDtypeStruct((n_rows, D), out_dtype),
        compiler_params=_cparams(("parallel",)),
    )(h, g, s)


def _flash_kernel(q_ref, k_ref, v_ref, o_ref, m_sc, acc_sc, *, tk):
    reps = tk // LANES
    m_sc[...] = jnp.full(m_sc.shape, -1e30, F32)
    acc_sc[...] = jnp.zeros(acc_sc.shape, F32)
    heads = range(G_C)
    tq = q_ref.shape[0]
    rh = min(tq, 256)
    units = [(g, r) for r in range(0, tq, rh) for g in heads]
    ahead = 4

    def kv_block(j, carry):
        off = pl.multiple_of(j * tk, tk)
        k = k_ref[pl.ds(off, tk), :]
        v = v_ref[pl.ds(off, tk), :]

        def scores(u):
            g, r = u
            return _nt(q_ref[r:r + rh, g * HEAD_C:(g + 1) * HEAD_C], k)

        def fold(u, s):
            g, r = u
            m_prev = m_sc[g, r:r + rh, :]
            m_new = jnp.maximum(m_prev, jnp.max(s, axis=-1, keepdims=True))
            p = jnp.exp2((s - jnp.concatenate([m_new] * reps, axis=1)).astype(BF16))
            alpha = jnp.exp2(m_prev - m_new)
            acc_sc[g, r:r + rh, :] = jnp.concatenate([alpha, alpha], axis=1) * acc_sc[g, r:r + rh, :] + _mm(p, v)
            m_sc[g, r:r + rh, :] = m_new

        pending = [scores(u) for u in units[:ahead]]
        for idx, u in enumerate(units):
            if idx + ahead < len(units):
                pending.append(scores(units[idx + ahead]))
            fold(u, pending[idx])
        return carry

    lax.fori_loop(0, k_ref.shape[0] // tk, kv_block, 0)
    for g in heads:
        a = acc_sc[g]
        o_ref[:, g * HEAD_C:(g + 1) * HEAD_C] = (a[:, :HEAD_C] / a[:, HEAD_C:]).astype(o_ref.dtype)


def flash_gqa(q, k, v_ext, q_rows, k_rows):
    q0, Lq = q_rows
    k0, Lk = k_rows
    assert k0 % Lk == 0
    tq = _pick(math.gcd(q0, Lq), (1024, 512, 256, 128))
    tk = _pick(Lk, (1280, 1024, 640, 512, 256, 128))
    qb, kb = q0 // tq, k0 // Lk
    gw = G_C * HEAD_C
    once = pl.Buffered(1)
    return pl.pallas_call(
        functools.partial(_flash_kernel, tk=tk),
        grid=(KV_C, Lq // tq),
        in_specs=[pl.BlockSpec((tq, gw), lambda n, i: (i + qb, n)),
                  pl.BlockSpec((Lk, HEAD_C), lambda n, i: (kb, n), pipeline_mode=once),
                  pl.BlockSpec((Lk, 2 * HEAD_C), lambda n, i: (kb, n), pipeline_mode=once)],
        out_specs=pl.BlockSpec((tq, gw), lambda n, i: (i, n)),
        out_shape=jax.ShapeDtypeStruct((Lq, H_C * HEAD_C), BF16),
        scratch_shapes=[pltpu.VMEM((G_C, tq, LANES), F32),
                        pltpu.VMEM((G_C, tq, 2 * HEAD_C), F32)],
        compiler_params=_cparams(("parallel", "arbitrary")),
    )(q, k, v_ext)


def _attn_prep_kernel(p_ref, cos_ref, sin_ref, qg_ref, kg_ref, q_ref, k_ref, v_ref):
    cos = cos_ref[...]
    sin = sin_ref[...]
    lane = lax.broadcasted_iota(jnp.int32, cos.shape, 1)
    first = (lane % (2 * ROPE_PAIRS)) < ROPE_PAIRS

    def norm_rope(x, gain):
        xn = x * lax.rsqrt(jnp.mean(x * x, axis=-1, keepdims=True) + EPS) * gain
        partner = jnp.where(first, pltpu.roll(xn, HEAD_C - ROPE_PAIRS, 1), pltpu.roll(xn, ROPE_PAIRS, 1))
        return xn * cos + partner * sin

    qg = qg_ref[...] * (LOG2E * HEAD_C ** -0.5)
    kg = kg_ref[...]
    for h in range(H_C):
        sl = slice(h * HEAD_C, (h + 1) * HEAD_C)
        q_ref[:, sl] = norm_rope(p_ref[:, sl], qg).astype(q_ref.dtype)
    ones = jnp.ones((p_ref.shape[0], HEAD_C), v_ref.dtype)
    for h in range(KV_C):
        sl = slice(h * HEAD_C, (h + 1) * HEAD_C)
        k_ref[:, sl] = norm_rope(p_ref[:, D_MODEL + h * HEAD_C:D_MODEL + (h + 1) * HEAD_C], kg).astype(k_ref.dtype)
        v_ref[:, 2 * h * HEAD_C:(2 * h + 1) * HEAD_C] = \
            p_ref[:, D_MODEL + KV_DIM + h * HEAD_C:D_MODEL + KV_DIM + (h + 1) * HEAD_C].astype(v_ref.dtype)
        v_ref[:, (2 * h + 1) * HEAD_C:(2 * h + 2) * HEAD_C] = ones


def attn_prep(p, cos, sin, q_gain, k_gain):
    L = p.shape[0]
    tm = _pick(L, (256, 128, 64, 32, 16, 8))
    row = lambda w: pl.BlockSpec((tm, w), lambda i: (i, 0))
    par = pl.BlockSpec((1, HEAD_C), lambda i: (0, 0))
    return pl.pallas_call(
        _attn_prep_kernel,
        grid=(L // tm,),
        in_specs=[row(D_MODEL + 2 * KV_DIM), row(HEAD_C), row(HEAD_C), par, par],
        out_specs=[row(D_MODEL), row(KV_DIM), row(2 * KV_DIM)],
        out_shape=[jax.ShapeDtypeStruct((L, D_MODEL), BF16),
                   jax.ShapeDtypeStruct((L, KV_DIM), BF16),
                   jax.ShapeDtypeStruct((L, 2 * KV_DIM), BF16)],
        compiler_params=_cparams(("parallel",)),
    )(p, cos, sin, q_gain, k_gain)


def _head_sum64(x):
    r_ = lax.broadcasted_iota(jnp.int32, (LANES, LANES), 0) // HEAD_A
    c_ = lax.broadcasted_iota(jnp.int32, (LANES, LANES), 1) // HEAD_A
    ones_bd = (r_ == c_).astype(BF16)
    hi = x.astype(BF16)
    lo = (x - hi.astype(F32)).astype(BF16)
    tiles = [_mm(hi[:, j:j + LANES], ones_bd) + _mm(lo[:, j:j + LANES], ones_bd) for j in range(0, x.shape[1], LANES)]
    return jnp.concatenate(tiles, axis=1)


def _ab_prep_kernel(x_ref, xp_ref, xn_ref, z_ref, zp_ref, zn_ref, mu_ref, w2_ref, w0_ref, a2_ref, a0_ref,
                    g2_ref, kk_ref, ga2_ref, gab_ref,
                    r_o, k_o, v_o, kkn_o, lw0_o, lw1_o, a0_o, a1_o, g_o, la0_o, la1_o, *, n_lat, n_rows):
    tm = x_ref.shape[0]
    i = pl.program_id(0)
    loc = lax.broadcasted_iota(jnp.int32, (tm, 1), 0)
    seg_first = (i * tm == 0) | (i * tm == n_lat)
    seg_last = ((i + 1) * tm == n_lat) | ((i + 1) * tm == n_rows)

    def prev(x, halo):
        edge = jnp.where(seg_first, 0.0, halo[SUBLANES - 1:SUBLANES, :])
        return jnp.where(loc == 0, edge, pltpu.roll(x, 1, 0))

    def nxt(x, halo):
        edge = jnp.where(seg_last, 0.0, halo[0:1, :])
        return jnp.where(loc == tm - 1, edge, pltpu.roll(x, tm - 1, 0))

    x = x_ref[...]
    mu = mu_ref[...]
    rkv = x + mu[0:1, :] * (prev(x, xp_ref[...]) - x) + mu[1:2, :] * (nxt(x, xn_ref[...]) - x)
    r_o[...] = rkv[:, :D_A]
    k = rkv[:, D_A:2 * D_A]
    k_o[...] = k
    v_o[...] = rkv[:, 2 * D_A:]
    kk = k * kk_ref[...]
    kkn_o[...] = kk / jnp.maximum(jnp.sqrt(_head_sum64(kk * kk)), 1e-12)

    def lora(off, width):
        c = z_ref[:, off:off + width]
        p_ = prev(z_ref[:, off + width:off + 2 * width], zp_ref[:, off + width:off + 2 * width])
        n_ = nxt(z_ref[:, off + 2 * width:off + 3 * width], zn_ref[:, off + 2 * width:off + 3 * width])
        return c + p_ + n_

    zw = jnp.tanh(lora(OFF_ZW, 2 * R_DECAY)).astype(BF16)
    za = lora(OFF_ZA, 2 * R_AAA).astype(BF16)
    zg = _sigmoid(lora(OFF_ZG, R_GATE)).astype(BF16)
    zga = z_ref[:, OFF_ZGA:OFF_ZGA + LANES].astype(BF16)
    g_o[...] = _mm(zg, g2_ref[...])
    for d, (lw_o, a_o, la_o) in enumerate(((lw0_o, a0_o, la0_o), (lw1_o, a1_o, la1_o))):
        lw = _mm(zw[:, d * R_DECAY:(d + 1) * R_DECAY], w2_ref[d]) + w0_ref[d:d + 1, :]
        lw_o[...] = -jnp.exp(-_softplus(-lw) - 0.5)
        a_o[...] = _sigmoid(_mm(za[:, d * R_AAA:(d + 1) * R_AAA], a2_ref[d]) + a0_ref[d:d + 1, :])
        zz = _mm(zga, ga2_ref[d]) + gab_ref[d:d + 1, :]
        la_o[...] = -_softplus(-zz) * (1.0 / GLA_GATE_NORM)


def ab_prep(p_all, P, n_lat):
    L = p_all.shape[0]
    tm = _row_tile(n_lat, L, (128, 64, 32, 16, 8))
    nb8 = tm // SUBLANES
    last8 = L // SUBLANES - 1
    wx, cz = 3 * D_A, D_AB_IN // LORA_W
    full = lambda shape: pl.BlockSpec(shape, lambda i: (0,) * len(shape))
    outw = lambda w: pl.BlockSpec((tm, w), lambda i: (i, 0))
    in_specs = [
        pl.BlockSpec((tm, wx), lambda i: (i, 0)),
        pl.BlockSpec((SUBLANES, wx), lambda i: (jnp.maximum(i * nb8 - 1, 0), 0)),
        pl.BlockSpec((SUBLANES, wx), lambda i: (jnp.minimum((i + 1) * nb8, last8), 0)),
        pl.BlockSpec((tm, LORA_W), lambda i: (i, cz)),
        pl.BlockSpec((SUBLANES, LORA_W), lambda i: (jnp.maximum(i * nb8 - 1, 0), cz)),
        pl.BlockSpec((SUBLANES, LORA_W), lambda i: (jnp.minimum((i + 1) * nb8, last8), cz)),
        full((2, wx)), full((2, R_DECAY, D_A)), full((2, D_A)), full((2, R_AAA, D_A)), full((2, D_A)),
        full((R_GATE, D_A)), full((1, D_A)), full((2, LANES, D_B_K)), full((2, D_B_K)),
    ]
    outs = [D_A] * 9 + [D_B_K] * 2
    return pl.pallas_call(
        functools.partial(_ab_prep_kernel, n_lat=n_lat, n_rows=L),
        grid=(L // tm,),
        in_specs=in_specs,
        out_specs=[outw(w) for w in outs],
        out_shape=[jax.ShapeDtypeStruct((L, w), F32) for w in outs],
        compiler_params=_cparams(("parallel",)),
    )(p_all, p_all, p_all, p_all, p_all, p_all, P['mu_rkv'], P['w2'], P['w0'], P['a2'], P['a0'],
      P['g2'], P['k_k'], P['ga2pad'], P['gab'])


def _ab_out_kernel(yf_ref, yb_ref, r_ref, k_ref, v_ref, a0_ref, a1_ref, g_ref, of_ref, ob_ref, go_ref,
                   rk_ref, ka_ref, lnw_ref, lnb_ref, gn_ref, o_ref):
    y = yf_ref[...] + yb_ref[...]
    mean = _head_sum64(y) * (1.0 / HEAD_A)
    d = y - mean
    var = _head_sum64(d * d) * (1.0 / HEAD_A)
    yn = d * lax.rsqrt(var + RWKV_LN_EPS) * lnw_ref[...] + lnb_ref[...]
    kdsum = k_ref[...] * (2.0 + (a0_ref[...] + a1_ref[...] - 2.0) * ka_ref[...])
    bonus = _head_sum64(r_ref[...] * kdsum * rk_ref[...]) * v_ref[...]
    o_ref[:, :D_A] = ((yn + bonus) * g_ref[...]).astype(o_ref.dtype)
    o = of_ref[...] + ob_ref[...]
    go = go_ref[...]
    for h in range(H_B):
        sl = slice(h * DV_B, (h + 1) * DV_B)
        oh = o[:, sl]
        yh = oh * lax.rsqrt(jnp.mean(oh * oh, axis=-1, keepdims=True) + EPS) * gn_ref[...]
        gh = go[:, sl]
        o_ref[:, D_A + h * DV_B:D_A + (h + 1) * DV_B] = (yh * gh * _sigmoid(gh)).astype(o_ref.dtype)


def ab_out(yf, yb, prep, of, ob, p_all, P, n_lat):
    L = yf.shape[0]
    tm = _row_tile(n_lat, L, (128, 64, 32, 16, 8))
    rw = pl.BlockSpec((tm, D_A), lambda i: (i, 0))
    par = lambda w: pl.BlockSpec((1, w), lambda i: (0, 0))
    go_spec = pl.BlockSpec((tm, D_B_V), lambda i: (i, (3 * D_A + 2 * D_B_K + D_B_V) // D_B_V))
    return pl.pallas_call(
        _ab_out_kernel,
        grid=(L // tm,),
        in_specs=[rw] * 10 + [go_spec, par(D_A), par(D_A), par(D_A), par(D_A), par(DV_B)],
        out_specs=pl.BlockSpec((tm, D_MODEL), lambda i: (i, 0)),
        out_shape=jax.ShapeDtypeStruct((L, D_MODEL), BF16),
        compiler_params=_cparams(("parallel",)),
    )(yf, yb, prep['r'], prep['k'], prep['v'], prep['a'][0], prep['a'][1], prep['g'], of, ob, p_all,
      P['r_k'], P['k_a'], P['ln_w'], P['ln_b'], P['g_norm'])


def _mm_split3(m, x):
    hi = x.astype(BF16)
    r1 = x - hi.astype(F32)
    mid = r1.astype(BF16)
    lo = (r1 - mid.astype(F32)).astype(BF16)
    return _mm(m, hi) + _mm(m, mid) + _mm(m, lo)


def _chunk_block(c, n_lat_chunks, n_ctx_chunks, rev):
    if rev:
        return n_lat_chunks + n_ctx_chunks - 1 - c
    return jnp.where(c < n_ctx_chunks, n_lat_chunks + c, c - n_ctx_chunks)


def _rwkv_kernel(r_ref, lw_ref, k_ref, v_ref, kk_ref, a_ref, ka_ref, y_ref, s_sc, *, npairs, rev):
    C = SCAN_CHUNK
    C2 = 2 * C

    @pl.when(pl.program_id(1) == 0)
    def _():
        s_sc[...] = jnp.zeros(s_sc.shape, F32)

    row = lax.broadcasted_iota(jnp.int32, (C, C), 0)
    col = lax.broadcasted_iota(jnp.int32, (C, C), 1)
    tri = ((col >= row) if rev else (col <= row)).astype(BF16)
    row2 = lax.broadcasted_iota(jnp.int32, (C2, C2), 0)
    col2 = lax.broadcasted_iota(jnp.int32, (C2, C2), 1)
    same = (row2 // C) == (col2 // C)
    strict2 = same & ((col2 > row2) if rev else (col2 < row2))
    incl2 = same & ((col2 >= row2) if rev else (col2 <= row2))
    eye2 = (row2 == col2).astype(F32)
    last = 0 if rev else C - 1
    head0 = lax.broadcasted_iota(jnp.int32, (C, LANES), 1) < HEAD_A

    lw = lw_ref[...]
    cl = _mm_split3(tri, lw)
    cl_last = cl[last:last + 1, :]
    kk = kk_ref[...]
    a = a_ref[...]
    k = k_ref[...] * (1.0 + (a - 1.0) * ka_ref[...])
    b = a * kk
    p_inv = jnp.exp(-cl)
    tail = jnp.exp(cl_last - cl)
    kkd = kk * jnp.exp(cl - lw)
    rd = r_ref[...] * jnp.exp(cl)
    kinv = k * p_inv
    binv = b * p_inv
    kdec = k * tail
    nbdec = -(b * tail)
    v = v_ref[...]
    decay_last = jnp.exp(cl_last)

    def stack(x, p):
        xs = x[:, p * LANES:(p + 1) * LANES]
        return jnp.concatenate([jnp.where(head0, xs, 0.0), jnp.where(head0, 0.0, xs)], axis=0).astype(BF16)

    pairs = range(npairs)
    cat0 = lambda a_, b_: jnp.concatenate([a_, b_], axis=0)
    lhs = [cat0(stack(kkd, p), stack(rd, p)) for p in pairs]
    rhs_g = [cat0(stack(kinv, p), stack(binv, p)) for p in pairs]
    dec2 = [cat0(stack(kdec, p), stack(nbdec, p)) for p in pairs]
    v2 = [stack(v, p) for p in pairs]
    G = [_nt(lhs[p], rhs_g[p]) for p in pairs]
    a_kk = [jnp.where(strict2, G[p][:C2, :C2], 0.0).astype(BF16) for p in pairs]
    n1 = [jnp.where(strict2, -G[p][:C2, C2:], 0.0).astype(BF16) for p in pairs]
    b_rkb = [jnp.concatenate([jnp.where(incl2, G[p][C2:, :C2], 0.0),
                              jnp.where(incl2, -G[p][C2:, C2:], 0.0)], axis=1).astype(BF16) for p in pairs]

    powers = [n1]
    for _ in range(5):
        prev = powers[-1]
        powers.append([_mm(prev[p], prev[p]).astype(BF16) for p in pairs])
    tinv = [eye2 + n1[p].astype(F32) for p in pairs]
    for npow in powers[1:]:
        tinv = [tinv[p] + _mm(npow[p], tinv[p].astype(BF16)) for p in pairs]
    tinv = [t.astype(BF16) for t in tinv]

    S = [s_sc[p] for p in pairs]
    inter = [_nt(lhs[p], S[p].astype(BF16)) for p in pairs]
    rhs = [inter[p][:C2] + _mm(a_kk[p], v2[p]) for p in pairs]
    uf = [_mm(tinv[p], rhs[p].astype(BF16)) for p in pairs]
    res = [rhs[p] - uf[p] + _mm(n1[p], uf[p].astype(BF16)) for p in pairs]
    u2 = [(uf[p] + _mm(tinv[p], res[p].astype(BF16))).astype(BF16) for p in pairs]
    vu = [cat0(v2[p], u2[p]) for p in pairs]
    y2 = [inter[p][C2:] + _mm(b_rkb[p], vu[p]) for p in pairs]
    for p in pairs:
        sl = slice(p * LANES, (p + 1) * LANES)
        y_ref[:, sl] = y2[p][:C] + y2[p][C:]
        s_sc[p] = S[p] * decay_last[:, sl] + _tn(vu[p], dec2[p])


def rwkv_scan(r, lw, k, v, kk, a, k_a, n_lat, rev):
    L = r.shape[0]
    C = SCAN_CHUNK
    gw = D_A
    npairs = gw // LANES
    blk = functools.partial(_chunk_block, n_lat_chunks=n_lat // C, n_ctx_chunks=(L - n_lat) // C, rev=rev)
    spec = pl.BlockSpec((C, gw), lambda g, c: (blk(c), g))
    return pl.pallas_call(
        functools.partial(_rwkv_kernel, npairs=npairs, rev=rev),
        grid=(D_A // gw, L // C),
        in_specs=[spec] * 6 + [pl.BlockSpec((1, gw), lambda g, c: (0, g))],
        out_specs=spec,
        out_shape=jax.ShapeDtypeStruct((L, D_A), F32),
        scratch_shapes=[pltpu.VMEM((npairs, LANES, LANES), F32)],
        compiler_params=_cparams(("parallel", "arbitrary")),
    )(r, lw, k, v, kk, a, k_a)


def _gla_kernel(q_ref, k_ref, v_ref, g_ref, o_ref, s_sc, *, rev):
    C = SCAN_CHUNK

    @pl.when(pl.program_id(0) == 0)
    def _():
        s_sc[...] = jnp.zeros(s_sc.shape, F32)

    row = lax.broadcasted_iota(jnp.int32, (C, C), 0)
    col = lax.broadcasted_iota(jnp.int32, (C, C), 1)
    le = (col >= row) if rev else (col <= row)
    trow = lax.broadcasted_iota(jnp.int32, (C, 1), 0)
    mats = [jnp.where(le, 1.0, 0.0)]
    levels = []
    n = C
    while n >= 2:
        half = n // 2
        same = (row // n) == (col // n)
        row_2nd = ((row % n) >= half) != rev
        col_2nd = ((col % n) >= half) != rev
        mats.append(jnp.where(same & row_2nd & col_2nd & le, 1.0, 0.0)
                    - jnp.where(same & (~row_2nd) & (~col_2nd) & (~le), 1.0, 0.0))
        levels.append((same, ((trow % n) >= half) != rev))
        n = half
    mall = jnp.concatenate(mats, axis=0).astype(BF16)

    call = _mm_split3(mall, g_ref[...])
    b = call[:C]
    last = 0 if rev else C - 1
    b_last = b[last:last + 1, :]
    q = q_ref[...] * (DK_B ** -0.5)
    k = k_ref[...]
    qe = (q * jnp.exp(b)).astype(BF16)
    kd = (k * jnp.exp(b_last - b)).astype(BF16)
    dec = jnp.exp(b_last)
    qk = q * k
    qts, kts = [], []
    for l, (_, second) in enumerate(levels):
        e = jnp.exp(-jnp.abs(call[(l + 1) * C:(l + 2) * C]))
        qts.append(jnp.where(second, q * e, 0.0).astype(BF16))
        kts.append(jnp.where(second, 0.0, k * e).astype(BF16))
    vb = v_ref[...].astype(BF16)

    heads = range(H_B)
    ks = [slice(h * DK_B, (h + 1) * DK_B) for h in heads]
    vs = [slice(h * DV_B, (h + 1) * DV_B) for h in heads]
    St = [s_sc[h] for h in heads]
    o_inter = [_nt(qe[:, ks[h]], St[h].astype(BF16)) for h in heads]
    att = [jnp.where(row == col, jnp.sum(qk[:, ks[h]], axis=-1, keepdims=True), 0.0) for h in heads]
    for l, (same, _) in enumerate(levels):
        att = [att[h] + jnp.where(same, _nt(qts[l][:, ks[h]], kts[l][:, ks[h]]), 0.0) for h in heads]
    for h in heads:
        o_ref[:, vs[h]] = o_inter[h] + _mm(att[h].astype(BF16), vb[:, vs[h]])
        s_sc[h] = St[h] * dec[:, ks[h]] + _tn(vb[:, vs[h]], kd[:, ks[h]])


def gla_scan(p_all, g, n_lat, rev):
    L = p_all.shape[0]
    C = SCAN_CHUNK
    blk = functools.partial(_chunk_block, n_lat_chunks=n_lat // C, n_ctx_chunks=(L - n_lat) // C, rev=rev)
    qc, vc = 3 * D_A // D_B_K, (3 * D_A + 2 * D_B_K) // D_B_V
    return pl.pallas_call(
        functools.partial(_gla_kernel, rev=rev),
        grid=(L // C,),
        in_specs=[pl.BlockSpec((C, D_B_K), lambda c: (blk(c), qc)),
                  pl.BlockSpec((C, D_B_K), lambda c: (blk(c), qc + 1)),
                  pl.BlockSpec((C, D_B_V), lambda c: (blk(c), vc)),
                  pl.BlockSpec((C, D_B_K), lambda c: (blk(c), 0))],
        out_specs=pl.BlockSpec((C, D_B_V), lambda c: (blk(c), 0)),
        out_shape=jax.ShapeDtypeStruct((L, D_B_V), F32),
        scratch_shapes=[pltpu.VMEM((H_B, DV_B, DK_B), F32)],
        compiler_params=_cparams(("arbitrary",)),
    )(p_all, p_all, p_all, g)


def _seg(m, j):
    return m[0:2, j * D_MODEL:(j + 1) * D_MODEL]


def _coef(rows2):
    return jnp.zeros((SUBLANES, D_MODEL), F32).at[0:2].set(rows2)


def _ffn_half(h, m, kidx, gain, w13, w2, lead, n_lat, n_rows):
    shift, scale, gate = _seg(m, 3 * kidx), _seg(m, 3 * kidx + 1), _seg(m, 3 * kidx + 2)
    u = normmod(h, (gain[None, :] * (1.0 + scale))[:, None, :], shift[:, None, :], n_lat, n_rows, BF16)
    act = matmul_swiglu(u, w13, lead)
    return matmul_residual(act, w2, h, _coef(FFN_RES * gate), n_lat, lead)


def _shifted_lora(w, mu):
    return jnp.concatenate([(1.0 - mu[0] - mu[1])[:, None] * w, mu[0][:, None] * w, mu[1][:, None] * w], axis=1)


def _mixer_ab(u, P, n_lat):
    p_all = matmul(u, P['w_big'])
    r, k, v, kk, lw0, lw1, a0, a1, g, la0, la1 = ab_prep(p_all, P, n_lat)
    prep = dict(r=r, k=k, v=v, a=(a0, a1), g=g)
    yf = rwkv_scan(r, lw0, k, v, kk, a0, P['k_a'], n_lat, False)
    yb = rwkv_scan(r, lw1, k, v, kk, a1, P['k_a'], n_lat, True)
    of = gla_scan(p_all, la0, n_lat, False)
    ob = gla_scan(p_all, la1, n_lat, True)
    return ab_out(yf, yb, prep, of, ob, p_all, P, n_lat)


def _rope_tables(n_lat, n_rows):
    rows = n_lat // GRID_W
    rr = jnp.repeat(jnp.arange(rows), GRID_W)
    cc = jnp.tile(jnp.arange(GRID_W), rows)
    inv_freq = ROPE_THETA ** (-jnp.arange(ROPE_PAIRS, dtype=F32) / ROPE_PAIRS)
    ar = rr.astype(F32)[:, None] * inv_freq
    ac = cc.astype(F32)[:, None] * inv_freq
    cos = jnp.concatenate([jnp.cos(ar), jnp.cos(ar), jnp.cos(ac), jnp.cos(ac)], axis=1)
    sin = jnp.concatenate([-jnp.sin(ar), jnp.sin(ar), -jnp.sin(ac), jnp.sin(ac)], axis=1)
    n_ctx = n_rows - n_lat
    cos = jnp.concatenate([cos, jnp.ones((n_ctx, HEAD_C), F32)], axis=0)
    sin = jnp.concatenate([sin, jnp.zeros((n_ctx, HEAD_C), F32)], axis=0)
    return cos, sin


def _mixer_c(u, P, n_lat, need_ctx):
    L = u.shape[0]
    p = matmul(u, P['w_in'], P['lead'])
    q, k, v_ext = attn_prep(p, P['cos'], P['sin'], P['q_gain'], P['k_gain'])
    o_l = flash_gqa(q, k, v_ext, (0, n_lat), (0, L))
    if not need_ctx:
        return o_l
    o_c = flash_gqa(q, k, v_ext, (n_lat, L - n_lat), (n_lat, L - n_lat))
    return jnp.concatenate([o_l, o_c], axis=0)


def kernel(x, c, ctx, c_ctx, ada_down, ada_up, ada_bias, norm_gains, final_gain, ffn_w13, ffn_w2, ab_w_in, ab_w_out, rwkv_mu_rkv, rwkv_mu_lora, rwkv_w1, rwkv_w2, rwkv_w0, rwkv_a1, rwkv_a2, rwkv_a0, rwkv_g1, rwkv_g2, rwkv_k_k, rwkv_k_a, rwkv_r_k, rwkv_ln_w, rwkv_ln_b, gla_a1, gla_a2, gla_ab, gla_norm, attn_w_in, attn_w_out, attn_q_norm, attn_k_norm):
    D = D_MODEL
    n_lat = x.shape[1]
    h = jnp.concatenate([x[0], ctx[0]], axis=0)
    n_rows = h.shape[0]
    bf = lambda t: t.astype(BF16)
    cvec = jnp.zeros((16, D), F32).at[0].set(c[0]).at[1].set(c_ctx)
    cs = bf(cvec * jax.nn.sigmoid(cvec))
    cos, sin = _rope_tables(n_lat, n_rows)
    cat1 = lambda ts: jnp.concatenate(ts, axis=1)
    w13_s, w2_s, ada_down_s, ada_up_s = bf(ffn_w13), bf(ffn_w2), bf(ada_down), bf(ada_up)
    ab_w_out_s, attn_w_in_s, attn_w_out_s = bf(ab_w_out), bf(attn_w_in), bf(attn_w_out)
    for layer in range(DEPTH):
        last = layer == DEPTH - 1
        i = layer // 2
        m = matmul(bf(matmul(cs, ada_down_s, (layer,))), ada_up_s, (layer,)) + ada_bias[layer]
        gains = norm_gains[layer]
        h = _ffn_half(h, m, 0, gains[0], w13_s, w2_s, (layer, 0), n_lat, n_rows)
        u = normmod(h, (gains[1][None] * (1.0 + _seg(m, 4)))[:, None, :], _seg(m, 3)[:, None, :], n_lat, n_rows, BF16)
        if layer % 2 == 0:
            mu = rwkv_mu_lora[i]
            rg = R_GLA_GATE
            lora = cat1([_shifted_lora(cat1([rwkv_w1[i, 0], rwkv_w1[i, 1]]), mu[0]),
                         _shifted_lora(cat1([rwkv_a1[i, 0], rwkv_a1[i, 1]]), mu[1]),
                         _shifted_lora(rwkv_g1[i], mu[2]),
                         gla_a1[i, 0], gla_a1[i, 1]])
            lora = jnp.pad(lora, ((0, 0), (0, LORA_W - lora.shape[1])))
            ga2pad = jnp.stack([jnp.pad(gla_a2[i, d], ((d * rg, LANES - (d + 1) * rg), (0, 0))) for d in range(2)])
            P = dict(
                w_big=bf(cat1([ab_w_in[i], lora])), mu_rkv=rwkv_mu_rkv[i],
                w2=bf(rwkv_w2[i]), w0=rwkv_w0[i], a2=bf(rwkv_a2[i]), a0=rwkv_a0[i], g2=bf(rwkv_g2[i]),
                k_k=rwkv_k_k[i][None], k_a=rwkv_k_a[i][None], r_k=rwkv_r_k[i].reshape(1, D_A),
                ln_w=rwkv_ln_w[i][None], ln_b=rwkv_ln_b[i][None],
                ga2pad=bf(ga2pad), gab=gla_ab[i], g_norm=gla_norm[i][None])
            y = _mixer_ab(u, P, n_lat)
            w_out = ab_w_out_s
        else:
            P = dict(w_in=attn_w_in_s, lead=(i,), q_gain=attn_q_norm[i][None], k_gain=attn_k_norm[i][None],
                     cos=cos, sin=sin)
            y = _mixer_c(u, P, n_lat, not last)
            w_out = attn_w_out_s
        if last:
            n_rows = n_lat
        h = matmul_residual(y, w_out, h, _coef(_seg(m, 5)), n_lat, (i,))
        h = _ffn_half(h, m, 2, gains[2], w13_s, w2_s, (layer, 1), n_lat, n_rows)
    ones2 = jnp.stack([final_gain, final_gain])[:, None, :]
    out = normmod(h, ones2, jnp.zeros((2, 1, D), F32), n_lat, n_lat, F32)
    return out[None]
```

```python
import functools
import math

import jax
import jax.numpy as jnp
from jax import lax
from jax.experimental import pallas as pl
from jax.experimental.pallas import tpu as pltpu

F32 = jnp.float32
BF16 = jnp.bfloat16

D_MODEL = 4096
DEPTH = 4
GRID_W = 64
N_MOD = 9
D_FF = 6144
FFN_RES = 0.5
EPS = 1e-6
D_A = D_MODEL // 2
HEAD_A = 64
H_A = D_A // HEAD_A
R_DECAY = 128
R_AAA = 128
R_GATE = 256
RWKV_LN_EPS = 64e-5
D_B_K = D_MODEL // 4
D_B_V = D_MODEL // 2
H_B = 4
DK_B = D_B_K // H_B
DV_B = D_B_V // H_B
R_GLA_GATE = 16
GLA_GATE_NORM = 16.0
D_AB_IN = 3 * D_A + 2 * D_B_K + 2 * D_B_V
HEAD_C = 128
H_C = D_MODEL // HEAD_C
KV_C = 8
G_C = H_C // KV_C
KV_DIM = KV_C * HEAD_C
ROPE_PAIRS = HEAD_C // 4
ROPE_THETA = 10000.0
LOG2E = 1.4426950408889634

LANES = 128
SUBLANES = 8
SCAN_CHUNK = 64
VMEM_LIMIT = 60 * 1024 * 1024

LORA_W = 3072
AB_COLS = D_AB_IN + LORA_W
OFF_ZW, OFF_ZA, OFF_ZG = 0, 3 * 2 * R_DECAY, 3 * 2 * R_DECAY + 3 * 2 * R_AAA
OFF_ZGA = OFF_ZG + 3 * R_GATE


def _pick(n, prefs):
    for p in prefs:
        if n % p == 0:
            return p
    return n


def _cparams(sem):
    return pltpu.CompilerParams(dimension_semantics=sem, vmem_limit_bytes=VMEM_LIMIT)


def _row_tile(n_lat, n_rows, prefs):
    return _pick(math.gcd(n_lat, n_rows - n_lat) if n_rows > n_lat else n_lat, prefs)


def _sigmoid(x):
    return 1.0 / (1.0 + jnp.exp(-x))


def _softplus(x):
    return jnp.maximum(x, 0.0) + jnp.log(1.0 + jnp.exp(-jnp.abs(x)))


def _nt(a, b):
    return lax.dot_general(a, b, (((1,), (1,)), ((), ())), preferred_element_type=F32)


def _tn(a, b):
    return lax.dot_general(a, b, (((0,), (0,)), ((), ())), preferred_element_type=F32)


def _mm(a, b):
    return jnp.dot(a, b, preferred_element_type=F32)


def _mm_kernel(x_ref, w_ref, o_ref):
    o_ref[...] = _mm(x_ref[...], w_ref[...]).astype(o_ref.dtype)


def _mm_swiglu_kernel(x_ref, wg_ref, wu_ref, o_ref):
    x = x_ref[...]
    g = _mm(x, wg_ref[...])
    u = _mm(x, wu_ref[...])
    o_ref[...] = (g * _sigmoid(g) * u).astype(o_ref.dtype)


def _mm_res_kernel(x_ref, w_ref, h_ref, c_ref, o_ref, *, n_lat):
    tm = x_ref.shape[0]
    rows = pl.program_id(0) * tm + lax.broadcasted_iota(jnp.int32, (tm, 1), 0)
    coef = jnp.where(rows >= n_lat, c_ref[1:2, :], c_ref[0:1, :])
    o_ref[...] = h_ref[...] + coef * _mm(x_ref[...], w_ref[...])


_TM_PREFS = (1280, 1024, 640, 512, 256, 128, 64, 32, 16)


def _wspec(w, lead, tn, col):
    lead = tuple(lead)
    K = w.shape[-2]
    return pl.BlockSpec((None,) * len(lead) + (K, tn), lambda i, j: lead + (0, col(j)))


def matmul(x, w, lead=(), out_dtype=F32):
    M, K = x.shape
    N = w.shape[-1]
    tm = _pick(M, _TM_PREFS)
    wide = tm * K * 2 <= 8 * 1024 * 1024
    tn = _pick(N, (1024, 512, 256, 128) if wide else (512, 256, 128))
    return pl.pallas_call(
        _mm_kernel,
        grid=(M // tm, N // tn),
        in_specs=[pl.BlockSpec((tm, K), lambda i, j: (i, 0)),
                  _wspec(w, lead, tn, lambda j: j)],
        out_specs=pl.BlockSpec((tm, tn), lambda i, j: (i, j)),
        out_shape=jax.ShapeDtypeStruct((M, N), out_dtype),
        compiler_params=_cparams(("parallel", "parallel")),
    )(x, w)


def matmul_swiglu(x, w13, lead=()):
    M, K = x.shape
    F = w13.shape[-1] // 2
    tm = _pick(M, _TM_PREFS)
    tn = _pick(F, (512, 256, 128))
    nf = F // tn
    return pl.pallas_call(
        _mm_swiglu_kernel,
        grid=(M // tm, nf),
        in_specs=[pl.BlockSpec((tm, K), lambda i, j: (i, 0)),
                  _wspec(w13, lead, tn, lambda j: j),
                  _wspec(w13, lead, tn, lambda j: j + nf)],
        out_specs=pl.BlockSpec((tm, tn), lambda i, j: (i, j)),
        out_shape=jax.ShapeDtypeStruct((M, F), BF16),
        compiler_params=_cparams(("parallel", "parallel")),
    )(x, w13, w13)


def matmul_residual(x, w, h, coef, n_lat, lead=()):
    M, K = x.shape
    N = w.shape[-1]
    tm = _pick(M, _TM_PREFS)
    tn = _pick(N, (512, 256, 128))
    xmode = None
    return pl.pallas_call(
        functools.partial(_mm_res_kernel, n_lat=n_lat),
        grid=(M // tm, N // tn),
        in_specs=[pl.BlockSpec((tm, K), lambda i, j: (i, 0), pipeline_mode=xmode),
                  _wspec(w, lead, tn, lambda j: j),
                  pl.BlockSpec((tm, tn), lambda i, j: (i, j)),
                  pl.BlockSpec((SUBLANES, tn), lambda i, j: (0, j))],
        out_specs=pl.BlockSpec((tm, tn), lambda i, j: (i, j)),
        out_shape=jax.ShapeDtypeStruct((M, N), F32),
        compiler_params=_cparams(("parallel", "parallel")),
    )(x, w, h, coef)


def _normmod_kernel(h_ref, g_ref, s_ref, o_ref):
    x = h_ref[...]
    y = x * lax.rsqrt(jnp.mean(x * x, axis=-1, keepdims=True) + EPS)
    o_ref[...] = (y * g_ref[...] + s_ref[...]).astype(o_ref.dtype)


def normmod(h, g, s, n_lat, n_rows, out_dtype):
    D = h.shape[1]
    tm = _row_tile(n_lat, n_rows, (256, 128, 64, 32, 16, 8))
    seg = lambda i: (jnp.where(i * tm >= n_lat, 1, 0), 0, 0)
    return pl.pallas_call(
        _normmod_kernel,
        grid=(n_rows // tm,),
        in_specs=[pl.BlockSpec((tm, D), lambda i: (i, 0)),
                  pl.BlockSpec((None, 1, D), seg),
                  pl.BlockSpec((None, 1, D), seg)],
        out_specs=pl.BlockSpec((tm, D), lambda i: (i, 0)),
        out_shape=jax.ShapeDtypeStruct((n_rows, D), out_dtype),
        compiler_params=_cparams(("parallel",)),
    )(h, g, s)


def _flash_kernel(q_ref, k_ref, v_ref, o_ref, m_sc, acc_sc, *, tk):
    reps = tk // LANES
    m_sc[...] = jnp.full(m_sc.shape, -1e30, F32)
    acc_sc[...] = jnp.zeros(acc_sc.shape, F32)
    heads = range(G_C)
    tq = q_ref.shape[0]
    rh = min(tq, 256)
    units = [(g, r) for r in range(0, tq, rh) for g in heads]
    ahead = 4

    def kv_block(j, carry):
        off = pl.multiple_of(j * tk, tk)
        k = k_ref[pl.ds(off, tk), :]
        v = v_ref[pl.ds(off, tk), :]

        def scores(u):
            g, r = u
            return _nt(q_ref[r:r + rh, g * HEAD_C:(g + 1) * HEAD_C], k)

        def fold(u, s):
            g, r = u
            m_prev = m_sc[g, r:r + rh, :]
            m_new = jnp.maximum(m_prev, jnp.max(s, axis=-1, keepdims=True))
            p = jnp.exp2((s - jnp.concatenate([m_new] * reps, axis=1)).astype(BF16))
            alpha = jnp.exp2(m_prev - m_new)
            acc_sc[g, r:r + rh, :] = jnp.concatenate([alpha, alpha], axis=1) * acc_sc[g, r:r + rh, :] + _mm(p, v)
            m_sc[g, r:r + rh, :] = m_new

        pending = [scores(u) for u in units[:ahead]]
        for idx, u in enumerate(units):
            if idx + ahead < len(units):
                pending.append(scores(units[idx + ahead]))
            fold(u, pending[idx])
        return carry

    lax.fori_loop(0, k_ref.shape[0] // tk, kv_block, 0)
    for g in heads:
        a = acc_sc[g]
        o_ref[:, g * HEAD_C:(g + 1) * HEAD_C] = (a[:, :HEAD_C] / a[:, HEAD_C:]).astype(o_ref.dtype)


def flash_gqa(q, k, v_ext, q_rows, k_rows):
    q0, Lq = q_rows
    k0, Lk = k_rows
    assert k0 % Lk == 0
    tq = _pick(math.gcd(q0, Lq), (1024, 512, 256, 128))
    tk = _pick(Lk, (3328, 1280, 1024, 640, 512, 256, 128))
    qb, kb = q0 // tq, k0 // Lk
    gw = G_C * HEAD_C
    once = pl.Buffered(1)
    return pl.pallas_call(
        functools.partial(_flash_kernel, tk=tk),
        grid=(KV_C, Lq // tq),
        in_specs=[pl.BlockSpec((tq, gw), lambda n, i: (i + qb, n)),
                  pl.BlockSpec((Lk, HEAD_C), lambda n, i: (kb, n), pipeline_mode=once),
                  pl.BlockSpec((Lk, 2 * HEAD_C), lambda n, i: (kb, n), pipeline_mode=once)],
        out_specs=pl.BlockSpec((tq, gw), lambda n, i: (i, n)),
        out_shape=jax.ShapeDtypeStruct((Lq, H_C * HEAD_C), BF16),
        scratch_shapes=[pltpu.VMEM((G_C, tq, LANES), F32),
                        pltpu.VMEM((G_C, tq, 2 * HEAD_C), F32)],
        compiler_params=_cparams(("parallel", "arbitrary")),
    )(q, k, v_ext)


def _attn_prep_kernel(p_ref, cos_ref, sin_ref, qg_ref, kg_ref, q_ref, k_ref, v_ref):
    cos = cos_ref[...]
    sin = sin_ref[...]
    lane = lax.broadcasted_iota(jnp.int32, cos.shape, 1)
    first = (lane % (2 * ROPE_PAIRS)) < ROPE_PAIRS

    def norm_rope(x, gain):
        xn = x * lax.rsqrt(jnp.mean(x * x, axis=-1, keepdims=True) + EPS) * gain
        partner = jnp.where(first, pltpu.roll(xn, HEAD_C - ROPE_PAIRS, 1), pltpu.roll(xn, ROPE_PAIRS, 1))
        return xn * cos + partner * sin

    qg = qg_ref[...] * (LOG2E * HEAD_C ** -0.5)
    kg = kg_ref[...]
    for h in range(H_C):
        sl = slice(h * HEAD_C, (h + 1) * HEAD_C)
        q_ref[:, sl] = norm_rope(p_ref[:, sl], qg).astype(q_ref.dtype)
    ones = jnp.ones((p_ref.shape[0], HEAD_C), v_ref.dtype)
    for h in range(KV_C):
        sl = slice(h * HEAD_C, (h + 1) * HEAD_C)
        k_ref[:, sl] = norm_rope(p_ref[:, D_MODEL + h * HEAD_C:D_MODEL + (h + 1) * HEAD_C], kg).astype(k_ref.dtype)
        v_ref[:, 2 * h * HEAD_C:(2 * h + 1) * HEAD_C] = \
            p_ref[:, D_MODEL + KV_DIM + h * HEAD_C:D_MODEL + KV_DIM + (h + 1) * HEAD_C].astype(v_ref.dtype)
        v_ref[:, (2 * h + 1) * HEAD_C:(2 * h + 2) * HEAD_C] = ones


def attn_prep(p, cos, sin, q_gain, k_gain):
    L = p.shape[0]
    tm = _pick(L, (256, 128, 64, 32, 16, 8))
    row = lambda w: pl.BlockSpec((tm, w), lambda i: (i, 0))
    par = pl.BlockSpec((1, HEAD_C), lambda i: (0, 0))
    return pl.pallas_call(
        _attn_prep_kernel,
        grid=(L // tm,),
        in_specs=[row(D_MODEL + 2 * KV_DIM), row(HEAD_C), row(HEAD_C), par, par],
        out_specs=[row(D_MODEL), row(KV_DIM), row(2 * KV_DIM)],
        out_shape=[jax.ShapeDtypeStruct((L, D_MODEL), BF16),
                   jax.ShapeDtypeStruct((L, KV_DIM), BF16),
                   jax.ShapeDtypeStruct((L, 2 * KV_DIM), BF16)],
        compiler_params=_cparams(("parallel",)),
    )(p, cos, sin, q_gain, k_gain)


def _head_sum64(x):
    r_ = lax.broadcasted_iota(jnp.int32, (LANES, LANES), 0) // HEAD_A
    c_ = lax.broadcasted_iota(jnp.int32, (LANES, LANES), 1) // HEAD_A
    ones_bd = (r_ == c_).astype(BF16)
    hi = x.astype(BF16)
    lo = (x - hi.astype(F32)).astype(BF16)
    tiles = [_mm(hi[:, j:j + LANES], ones_bd) + _mm(lo[:, j:j + LANES], ones_bd) for j in range(0, x.shape[1], LANES)]
    return jnp.concatenate(tiles, axis=1)


def _ab_prep_kernel(x_ref, xp_ref, xn_ref, z_ref, zp_ref, zn_ref, mu_ref, w2_ref, w0_ref, a2_ref, a0_ref,
                    g2_ref, kk_ref, ga2_ref, gab_ref,
                    r_o, k_o, v_o, kkn_o, lw0_o, lw1_o, a0_o, a1_o, g_o, la0_o, la1_o, *, n_lat, n_rows):
    tm = x_ref.shape[0]
    i = pl.program_id(0)
    loc = lax.broadcasted_iota(jnp.int32, (tm, 1), 0)
    seg_first = (i * tm == 0) | (i * tm == n_lat)
    seg_last = ((i + 1) * tm == n_lat) | ((i + 1) * tm == n_rows)

    def prev(x, halo):
        edge = jnp.where(seg_first, 0.0, halo[SUBLANES - 1:SUBLANES, :])
        return jnp.where(loc == 0, edge, pltpu.roll(x, 1, 0))

    def nxt(x, halo):
        edge = jnp.where(seg_last, 0.0, halo[0:1, :])
        return jnp.where(loc == tm - 1, edge, pltpu.roll(x, tm - 1, 0))

    x = x_ref[...]
    mu = mu_ref[...]
    rkv = x + mu[0:1, :] * (prev(x, xp_ref[...]) - x) + mu[1:2, :] * (nxt(x, xn_ref[...]) - x)
    r_o[...] = rkv[:, :D_A]
    k = rkv[:, D_A:2 * D_A]
    k_o[...] = k
    v_o[...] = rkv[:, 2 * D_A:]
    kk = k * kk_ref[...]
    kkn_o[...] = kk / jnp.maximum(jnp.sqrt(_head_sum64(kk * kk)), 1e-12)

    def lora(off, width):
        c = z_ref[:, off:off + width]
        p_ = prev(z_ref[:, off + width:off + 2 * width], zp_ref[:, off + width:off + 2 * width])
        n_ = nxt(z_ref[:, off + 2 * width:off + 3 * width], zn_ref[:, off + 2 * width:off + 3 * width])
        return c + p_ + n_

    zw = jnp.tanh(lora(OFF_ZW, 2 * R_DECAY)).astype(BF16)
    za = lora(OFF_ZA, 2 * R_AAA).astype(BF16)
    zg = _sigmoid(lora(OFF_ZG, R_GATE)).astype(BF16)
    zga = z_ref[:, OFF_ZGA:OFF_ZGA + LANES].astype(BF16)
    g_o[...] = _mm(zg, g2_ref[...])
    for d, (lw_o, a_o, la_o) in enumerate(((lw0_o, a0_o, la0_o), (lw1_o, a1_o, la1_o))):
        lw = _mm(zw[:, d * R_DECAY:(d + 1) * R_DECAY], w2_ref[d]) + w0_ref[d:d + 1, :]
        lw_o[...] = -jnp.exp(-_softplus(-lw) - 0.5)
        a_o[...] = _sigmoid(_mm(za[:, d * R_AAA:(d + 1) * R_AAA], a2_ref[d]) + a0_ref[d:d + 1, :])
        zz = _mm(zga, ga2_ref[d]) + gab_ref[d:d + 1, :]
        la_o[...] = -_softplus(-zz) * (1.0 / GLA_GATE_NORM)


def ab_prep(p_all, P, n_lat):
    L = p_all.shape[0]
    tm = _row_tile(n_lat, L, (128, 64, 32, 16, 8))
    nb8 = tm // SUBLANES
    last8 = L // SUBLANES - 1
    wx, cz = 3 * D_A, D_AB_IN // LORA_W
    full = lambda shape: pl.BlockSpec(shape, lambda i: (0,) * len(shape))
    outw = lambda w: pl.BlockSpec((tm, w), lambda i: (i, 0))
    in_specs = [
        pl.BlockSpec((tm, wx), lambda i: (i, 0)),
        pl.BlockSpec((SUBLANES, wx), lambda i: (jnp.maximum(i * nb8 - 1, 0), 0)),
        pl.BlockSpec((SUBLANES, wx), lambda i: (jnp.minimum((i + 1) * nb8, last8), 0)),
        pl.BlockSpec((tm, LORA_W), lambda i: (i, cz)),
        pl.BlockSpec((SUBLANES, LORA_W), lambda i: (jnp.maximum(i * nb8 - 1, 0), cz)),
        pl.BlockSpec((SUBLANES, LORA_W), lambda i: (jnp.minimum((i + 1) * nb8, last8), cz)),
        full((2, wx)), full((2, R_DECAY, D_A)), full((2, D_A)), full((2, R_AAA, D_A)), full((2, D_A)),
        full((R_GATE, D_A)), full((1, D_A)), full((2, LANES, D_B_K)), full((2, D_B_K)),
    ]
    outs = [D_A] * 9 + [D_B_K] * 2
    return pl.pallas_call(
        functools.partial(_ab_prep_kernel, n_lat=n_lat, n_rows=L),
        grid=(L // tm,),
        in_specs=in_specs,
        out_specs=[outw(w) for w in outs],
        out_shape=[jax.ShapeDtypeStruct((L, w), F32) for w in outs],
        compiler_params=_cparams(("parallel",)),
    )(p_all, p_all, p_all, p_all, p_all, p_all, P['mu_rkv'], P['w2'], P['w0'], P['a2'], P['a0'],
      P['g2'], P['k_k'], P['ga2pad'], P['gab'])


def _ab_out_kernel(yf_ref, yb_ref, r_ref, k_ref, v_ref, a0_ref, a1_ref, g_ref, of_ref, ob_ref, go_ref,
                   rk_ref, ka_ref, lnw_ref, lnb_ref, gn_ref, o_ref):
    y = yf_ref[...] + yb_ref[...]
    mean = _head_sum64(y) * (1.0 / HEAD_A)
    d = y - mean
    var = _head_sum64(d * d) * (1.0 / HEAD_A)
    yn = d * lax.rsqrt(var + RWKV_LN_EPS) * lnw_ref[...] + lnb_ref[...]
    kdsum = k_ref[...] * (2.0 + (a0_ref[...] + a1_ref[...] - 2.0) * ka_ref[...])
    bonus = _head_sum64(r_ref[...] * kdsum * rk_ref[...]) * v_ref[...]
    o_ref[:, :D_A] = ((yn + bonus) * g_ref[...]).astype(o_ref.dtype)
    o = of_ref[...] + ob_ref[...]
    go = go_ref[...]
    for h in range(H_B):
        sl = slice(h * DV_B, (h + 1) * DV_B)
        oh = o[:, sl]
        yh = oh * lax.rsqrt(jnp.mean(oh * oh, axis=-1, keepdims=True) + EPS) * gn_ref[...]
        gh = go[:, sl]
        o_ref[:, D_A + h * DV_B:D_A + (h + 1) * DV_B] = (yh * gh * _sigmoid(gh)).astype(o_ref.dtype)


def ab_out(yf, yb, prep, of, ob, p_all, P, n_lat):
    L = yf.shape[0]
    tm = _row_tile(n_lat, L, (128, 64, 32, 16, 8))
    rw = pl.BlockSpec((tm, D_A), lambda i: (i, 0))
    par = lambda w: pl.BlockSpec((1, w), lambda i: (0, 0))
    go_spec = pl.BlockSpec((tm, D_B_V), lambda i: (i, (3 * D_A + 2 * D_B_K + D_B_V) // D_B_V))
    return pl.pallas_call(
        _ab_out_kernel,
        grid=(L // tm,),
        in_specs=[rw] * 10 + [go_spec, par(D_A), par(D_A), par(D_A), par(D_A), par(DV_B)],
        out_specs=pl.BlockSpec((tm, D_MODEL), lambda i: (i, 0)),
        out_shape=jax.ShapeDtypeStruct((L, D_MODEL), BF16),
        compiler_params=_cparams(("parallel",)),
    )(yf, yb, prep['r'], prep['k'], prep['v'], prep['a'][0], prep['a'][1], prep['g'], of, ob, p_all,
      P['r_k'], P['k_a'], P['ln_w'], P['ln_b'], P['g_norm'])


def _mm_split3(m, x):
    hi = x.astype(BF16)
    r1 = x - hi.astype(F32)
    mid = r1.astype(BF16)
    lo = (r1 - mid.astype(F32)).astype(BF16)
    return _mm(m, hi) + _mm(m, mid) + _mm(m, lo)


def _chunk_block(c, n_lat_chunks, n_ctx_chunks, rev):
    if rev:
        return n_lat_chunks + n_ctx_chunks - 1 - c
    return jnp.where(c < n_ctx_chunks, n_lat_chunks + c, c - n_ctx_chunks)


def _rwkv_kernel(r_ref, lw_ref, k_ref, v_ref, kk_ref, a_ref, ka_ref, y_ref, s_sc, *, npairs, rev):
    C = SCAN_CHUNK
    C2 = 2 * C

    @pl.when(pl.program_id(1) == 0)
    def _():
        s_sc[...] = jnp.zeros(s_sc.shape, F32)

    row = lax.broadcasted_iota(jnp.int32, (C, C), 0)
    col = lax.broadcasted_iota(jnp.int32, (C, C), 1)
    tri = ((col >= row) if rev else (col <= row)).astype(BF16)
    row2 = lax.broadcasted_iota(jnp.int32, (C2, C2), 0)
    col2 = lax.broadcasted_iota(jnp.int32, (C2, C2), 1)
    same = (row2 // C) == (col2 // C)
    strict2 = same & ((col2 > row2) if rev else (col2 < row2))
    incl2 = same & ((col2 >= row2) if rev else (col2 <= row2))
    eye2 = (row2 == col2).astype(F32)
    last = 0 if rev else C - 1
    head0 = lax.broadcasted_iota(jnp.int32, (C, LANES), 1) < HEAD_A

    lw = lw_ref[...]
    cl = _mm_split3(tri, lw)
    cl_last = cl[last:last + 1, :]
    kk = kk_ref[...]
    a = a_ref[...]
    k = k_ref[...] * (1.0 + (a - 1.0) * ka_ref[...])
    b = a * kk
    p_inv = jnp.exp(-cl)
    tail = jnp.exp(cl_last - cl)
    kkd = kk * jnp.exp(cl - lw)
    rd = r_ref[...] * jnp.exp(cl)
    kinv = k * p_inv
    binv = b * p_inv
    kdec = k * tail
    nbdec = -(b * tail)
    v = v_ref[...]
    decay_last = jnp.exp(cl_last)

    def stack(x, p):
        xs = x[:, p * LANES:(p + 1) * LANES]
        return jnp.concatenate([jnp.where(head0, xs, 0.0), jnp.where(head0, 0.0, xs)], axis=0).astype(BF16)

    pairs = range(npairs)
    cat0 = lambda a_, b_: jnp.concatenate([a_, b_], axis=0)
    lhs = [cat0(stack(kkd, p), stack(rd, p)) for p in pairs]
    rhs_g = [cat0(stack(kinv, p), stack(binv, p)) for p in pairs]
    dec2 = [cat0(stack(kdec, p), stack(nbdec, p)) for p in pairs]
    v2 = [stack(v, p) for p in pairs]
    G = [_nt(lhs[p], rhs_g[p]) for p in pairs]
    a_kk = [jnp.where(strict2, G[p][:C2, :C2], 0.0).astype(BF16) for p in pairs]
    n1 = [jnp.where(strict2, -G[p][:C2, C2:], 0.0).astype(BF16) for p in pairs]
    b_rkb = [jnp.concatenate([jnp.where(incl2, G[p][C2:, :C2], 0.0),
                              jnp.where(incl2, -G[p][C2:, C2:], 0.0)], axis=1).astype(BF16) for p in pairs]

    powers = [n1]
    for _ in range(5):
        prev = powers[-1]
        powers.append([_mm(prev[p], prev[p]).astype(BF16) for p in pairs])
    tinv = [eye2 + n1[p].astype(F32) for p in pairs]
    for npow in powers[1:]:
        tinv = [tinv[p] + _mm(npow[p], tinv[p].astype(BF16)) for p in pairs]
    tinv = [t.astype(BF16) for t in tinv]

    S = [s_sc[p] for p in pairs]
    inter = [_nt(lhs[p], S[p].astype(BF16)) for p in pairs]
    rhs = [inter[p][:C2] + _mm(a_kk[p], v2[p]) for p in pairs]
    uf = [_mm(tinv[p], rhs[p].astype(BF16)) for p in pairs]
    res = [rhs[p] - uf[p] + _mm(n1[p], uf[p].astype(BF16)) for p in pairs]
    u2 = [(uf[p] + _mm(tinv[p], res[p].astype(BF16))).astype(BF16) for p in pairs]
    vu = [cat0(v2[p], u2[p]) for p in pairs]
    y2 = [inter[p][C2:] + _mm(b_rkb[p], vu[p]) for p in pairs]
    for p in pairs:
        sl = slice(p * LANES, (p + 1) * LANES)
        y_ref[:, sl] = y2[p][:C] + y2[p][C:]
        s_sc[p] = S[p] * decay_last[:, sl] + _tn(vu[p], dec2[p])


def rwkv_scan(r, lw, k, v, kk, a, k_a, n_lat, rev):
    L = r.shape[0]
    C = SCAN_CHUNK
    gw = D_A
    npairs = gw // LANES
    blk = functools.partial(_chunk_block, n_lat_chunks=n_lat // C, n_ctx_chunks=(L - n_lat) // C, rev=rev)
    spec = pl.BlockSpec((C, gw), lambda g, c: (blk(c), g))
    return pl.pallas_call(
        functools.partial(_rwkv_kernel, npairs=npairs, rev=rev),
        grid=(D_A // gw, L // C),
        in_specs=[spec] * 6 + [pl.BlockSpec((1, gw), lambda g, c: (0, g))],
        out_specs=spec,
        out_shape=jax.ShapeDtypeStruct((L, D_A), F32),
        scratch_shapes=[pltpu.VMEM((npairs, LANES, LANES), F32)],
        compiler_params=_cparams(("parallel", "arbitrary")),
    )(r, lw, k, v, kk, a, k_a)


def _gla_kernel(q_ref, k_ref, v_ref, g_ref, o_ref, s_sc, *, rev):
    C = SCAN_CHUNK

    @pl.when(pl.program_id(0) == 0)
    def _():
        s_sc[...] = jnp.zeros(s_sc.shape, F32)

    row = lax.broadcasted_iota(jnp.int32, (C, C), 0)
    col = lax.broadcasted_iota(jnp.int32, (C, C), 1)
    le = (col >= row) if rev else (col <= row)
    trow = lax.broadcasted_iota(jnp.int32, (C, 1), 0)
    mats = [jnp.where(le, 1.0, 0.0)]
    levels = []
    n = C
    while n >= 2:
        half = n // 2
        same = (row // n) == (col // n)
        row_2nd = ((row % n) >= half) != rev
        col_2nd = ((col % n) >= half) != rev
        mats.append(jnp.where(same & row_2nd & col_2nd & le, 1.0, 0.0)
                    - jnp.where(same & (~row_2nd) & (~col_2nd) & (~le), 1.0, 0.0))
        levels.append((same, ((trow % n) >= half) != rev))
        n = half
    mall = jnp.concatenate(mats, axis=0).astype(BF16)

    call = _mm_split3(mall, g_ref[...])
    b = call[:C]
    last = 0 if rev else C - 1
    b_last = b[last:last + 1, :]
    q = q_ref[...] * (DK_B ** -0.5)
    k = k_ref[...]
    qe = (q * jnp.exp(b)).astype(BF16)
    kd = (k * jnp.exp(b_last - b)).astype(BF16)
    dec = jnp.exp(b_last)
    qk = q * k
    qts, kts = [], []
    for l, (_, second) in enumerate(levels):
        e = jnp.exp(-jnp.abs(call[(l + 1) * C:(l + 2) * C]))
        qts.append(jnp.where(second, q * e, 0.0).astype(BF16))
        kts.append(jnp.where(second, 0.0, k * e).astype(BF16))
    vb = v_ref[...].astype(BF16)

    heads = range(H_B)
    ks = [slice(h * DK_B, (h + 1) * DK_B) for h in heads]
    vs = [slice(h * DV_B, (h + 1) * DV_B) for h in heads]
    St = [s_sc[h] for h in heads]
    o_inter = [_nt(qe[:, ks[h]], St[h].astype(BF16)) for h in heads]
    att = [jnp.where(row == col, jnp.sum(qk[:, ks[h]], axis=-1, keepdims=True), 0.0) for h in heads]
    for l, (same, _) in enumerate(levels):
        att = [att[h] + jnp.where(same, _nt(qts[l][:, ks[h]], kts[l][:, ks[h]]), 0.0) for h in heads]
    for h in heads:
        o_ref[:, vs[h]] = o_inter[h] + _mm(att[h].astype(BF16), vb[:, vs[h]])
        s_sc[h] = St[h] * dec[:, ks[h]] + _tn(vb[:, vs[h]], kd[:, ks[h]])


def gla_scan(p_all, g, n_lat, rev):
    L = p_all.shape[0]
    C = SCAN_CHUNK
    blk = functools.partial(_chunk_block, n_lat_chunks=n_lat // C, n_ctx_chunks=(L - n_lat) // C, rev=rev)
    qc, vc = 3 * D_A // D_B_K, (3 * D_A + 2 * D_B_K) // D_B_V
    return pl.pallas_call(
        functools.partial(_gla_kernel, rev=rev),
        grid=(L // C,),
        in_specs=[pl.BlockSpec((C, D_B_K), lambda c: (blk(c), qc)),
                  pl.BlockSpec((C, D_B_K), lambda c: (blk(c), qc + 1)),
                  pl.BlockSpec((C, D_B_V), lambda c: (blk(c), vc)),
                  pl.BlockSpec((C, D_B_K), lambda c: (blk(c), 0))],
        out_specs=pl.BlockSpec((C, D_B_V), lambda c: (blk(c), 0)),
        out_shape=jax.ShapeDtypeStruct((L, D_B_V), F32),
        scratch_shapes=[pltpu.VMEM((H_B, DV_B, DK_B), F32)],
        compiler_params=_cparams(("arbitrary",)),
    )(p_all, p_all, p_all, g)


def _seg(m, j):
    return m[0:2, j * D_MODEL:(j + 1) * D_MODEL]


def _coef(rows2):
    return jnp.zeros((SUBLANES, D_MODEL), F32).at[0:2].set(rows2)


def _ffn_half(h, m, kidx, gain, w13, w2, lead, n_lat, n_rows):
    shift, scale, gate = _seg(m, 3 * kidx), _seg(m, 3 * kidx + 1), _seg(m, 3 * kidx + 2)
    u = normmod(h, (gain[None, :] * (1.0 + scale))[:, None, :], shift[:, None, :], n_lat, n_rows, BF16)
    act = matmul_swiglu(u, w13, lead)
    return matmul_residual(act, w2, h, _coef(FFN_RES * gate), n_lat, lead)


def _shifted_lora(w, mu):
    return jnp.concatenate([(1.0 - mu[0] - mu[1])[:, None] * w, mu[0][:, None] * w, mu[1][:, None] * w], axis=1)


def _mixer_ab(u, P, n_lat):
    p_all = matmul(u, P['w_big'])
    r, k, v, kk, lw0, lw1, a0, a1, g, la0, la1 = ab_prep(p_all, P, n_lat)
    prep = dict(r=r, k=k, v=v, a=(a0, a1), g=g)
    yf = rwkv_scan(r, lw0, k, v, kk, a0, P['k_a'], n_lat, False)
    yb = rwkv_scan(r, lw1, k, v, kk, a1, P['k_a'], n_lat, True)
    of = gla_scan(p_all, la0, n_lat, False)
    ob = gla_scan(p_all, la1, n_lat, True)
    return ab_out(yf, yb, prep, of, ob, p_all, P, n_lat)


def _rope_tables(n_lat, n_rows):
    rows = n_lat // GRID_W
    rr = jnp.repeat(jnp.arange(rows), GRID_W)
    cc = jnp.tile(jnp.arange(GRID_W), rows)
    inv_freq = ROPE_THETA ** (-jnp.arange(ROPE_PAIRS, dtype=F32) / ROPE_PAIRS)
    ar = rr.astype(F32)[:, None] * inv_freq
    ac = cc.astype(F32)[:, None] * inv_freq
    cos = jnp.concatenate([jnp.cos(ar), jnp.cos(ar), jnp.cos(ac), jnp.cos(ac)], axis=1)
    sin = jnp.concatenate([-jnp.sin(ar), jnp.sin(ar), -jnp.sin(ac), jnp.sin(ac)], axis=1)
    n_ctx = n_rows - n_lat
    cos = jnp.concatenate([cos, jnp.ones((n_ctx, HEAD_C), F32)], axis=0)
    sin = jnp.concatenate([sin, jnp.zeros((n_ctx, HEAD_C), F32)], axis=0)
    return cos, sin


def _mixer_c(u, P, n_lat, need_ctx):
    L = u.shape[0]
    p = matmul(u, P['w_in'], P['lead'])
    q, k, v_ext = attn_prep(p, P['cos'], P['sin'], P['q_gain'], P['k_gain'])
    o_l = flash_gqa(q, k, v_ext, (0, n_lat), (0, L))
    if not need_ctx:
        return o_l
    o_c = flash_gqa(q, k, v_ext, (n_lat, L - n_lat), (n_lat, L - n_lat))
    return jnp.concatenate([o_l, o_c], axis=0)


def kernel(x, c, ctx, c_ctx, ada_down, ada_up, ada_bias, norm_gains, final_gain, ffn_w13, ffn_w2, ab_w_in, ab_w_out, rwkv_mu_rkv, rwkv_mu_lora, rwkv_w1, rwkv_w2, rwkv_w0, rwkv_a1, rwkv_a2, rwkv_a0, rwkv_g1, rwkv_g2, rwkv_k_k, rwkv_k_a, rwkv_r_k, rwkv_ln_w, rwkv_ln_b, gla_a1, gla_a2, gla_ab, gla_norm, attn_w_in, attn_w_out, attn_q_norm, attn_k_norm):
    D = D_MODEL
    n_lat = x.shape[1]
    h = jnp.concatenate([x[0], ctx[0]], axis=0)
    n_rows = h.shape[0]
    bf = lambda t: t.astype(BF16)
    cvec = jnp.zeros((16, D), F32).at[0].set(c[0]).at[1].set(c_ctx)
    cs = bf(cvec * jax.nn.sigmoid(cvec))
    cos, sin = _rope_tables(n_lat, n_rows)
    cat1 = lambda ts: jnp.concatenate(ts, axis=1)
    w13_s, w2_s, ada_down_s, ada_up_s = bf(ffn_w13), bf(ffn_w2), bf(ada_down), bf(ada_up)
    ab_w_out_s, attn_w_in_s, attn_w_out_s = bf(ab_w_out), bf(attn_w_in), bf(attn_w_out)
    for layer in range(DEPTH):
        last = layer == DEPTH - 1
        i = layer // 2
        m = matmul(bf(matmul(cs, ada_down_s, (layer,))), ada_up_s, (layer,)) + ada_bias[layer]
        gains = norm_gains[layer]
        h = _ffn_half(h, m, 0, gains[0], w13_s, w2_s, (layer, 0), n_lat, n_rows)
        u = normmod(h, (gains[1][None] * (1.0 + _seg(m, 4)))[:, None, :], _seg(m, 3)[:, None, :], n_lat, n_rows, BF16)
        if layer % 2 == 0:
            mu = rwkv_mu_lora[i]
            rg = R_GLA_GATE
            lora = cat1([_shifted_lora(cat1([rwkv_w1[i, 0], rwkv_w1[i, 1]]), mu[0]),
                         _shifted_lora(cat1([rwkv_a1[i, 0], rwkv_a1[i, 1]]), mu[1]),
                         _shifted_lora(rwkv_g1[i], mu[2]),
                         gla_a1[i, 0], gla_a1[i, 1]])
            lora = jnp.pad(lora, ((0, 0), (0, LORA_W - lora.shape[1])))
            ga2pad = jnp.stack([jnp.pad(gla_a2[i, d], ((d * rg, LANES - (d + 1) * rg), (0, 0))) for d in range(2)])
            P = dict(
                w_big=bf(cat1([ab_w_in[i], lora])), mu_rkv=rwkv_mu_rkv[i],
                w2=bf(rwkv_w2[i]), w0=rwkv_w0[i], a2=bf(rwkv_a2[i]), a0=rwkv_a0[i], g2=bf(rwkv_g2[i]),
                k_k=rwkv_k_k[i][None], k_a=rwkv_k_a[i][None], r_k=rwkv_r_k[i].reshape(1, D_A),
                ln_w=rwkv_ln_w[i][None], ln_b=rwkv_ln_b[i][None],
                ga2pad=bf(ga2pad), gab=gla_ab[i], g_norm=gla_norm[i][None])
            y = _mixer_ab(u, P, n_lat)
            w_out = ab_w_out_s
        else:
            P = dict(w_in=attn_w_in_s, lead=(i,), q_gain=attn_q_norm[i][None], k_gain=attn_k_norm[i][None],
                     cos=cos, sin=sin)
            y = _mixer_c(u, P, n_lat, not last)
            w_out = attn_w_out_s
        if last:
            n_rows = n_lat
        h = matmul_residual(y, w_out, h, _coef(_seg(m, 5)), n_lat, (i,))
        h = _ffn_half(h, m, 2, gains[2], w13_s, w2_s, (layer, 1), n_lat, n_rows)
    ones2 = jnp.stack([final_gain, final_gain])[:, None, :]
    out = normmod(h, ones2, jnp.zeros((2, 1, D), F32), n_lat, n_lat, F32)
    return out[None]
```

```python
import functools
import math

import jax
import jax.numpy as jnp
from jax import lax
from jax.experimental import pallas as pl
from jax.experimental.pallas import tpu as pltpu

F32 = jnp.float32
BF16 = jnp.bfloat16

D_MODEL = 4096
DEPTH = 4
GRID_W = 64
N_MOD = 9
D_FF = 6144
FFN_RES = 0.5
EPS = 1e-6
D_A = D_MODEL // 2
HEAD_A = 64
H_A = D_A // HEAD_A
R_DECAY = 128
R_AAA = 128
R_GATE = 256
RWKV_LN_EPS = 64e-5
D_B_K = D_MODEL // 4
D_B_V = D_MODEL // 2
H_B = 4
DK_B = D_B_K // H_B
DV_B = D_B_V // H_B
R_GLA_GATE = 16
GLA_GATE_NORM = 16.0
D_AB_IN = 3 * D_A + 2 * D_B_K + 2 * D_B_V
HEAD_C = 128
H_C = D_MODEL // HEAD_C
KV_C = 8
G_C = H_C // KV_C
KV_DIM = KV_C * HEAD_C
ROPE_PAIRS = HEAD_C // 4
ROPE_THETA = 10000.0
LOG2E = 1.4426950408889634

LANES = 128
SUBLANES = 8
SCAN_CHUNK = 64
VMEM_LIMIT = 60 * 1024 * 1024

LORA_W = 3072
AB_COLS = D_AB_IN + LORA_W
OFF_ZW, OFF_ZA, OFF_ZG = 0, 3 * 2 * R_DECAY, 3 * 2 * R_DECAY + 3 * 2 * R_AAA
OFF_ZGA = OFF_ZG + 3 * R_GATE


def _pick(n, prefs):
    for p in prefs:
        if n % p == 0:
            return p
    return n


def _cparams(sem):
    return pltpu.CompilerParams(dimension_semantics=sem, vmem_limit_bytes=VMEM_LIMIT)


def _row_tile(n_lat, n_rows, prefs):
    return _pick(math.gcd(n_lat, n_rows - n_lat) if n_rows > n_lat else n_lat, prefs)


def _sigmoid(x):
    return 1.0 / (1.0 + jnp.exp(-x))


def _softplus(x):
    return jnp.maximum(x, 0.0) + jnp.log(1.0 + jnp.exp(-jnp.abs(x)))


def _nt(a, b):
    return lax.dot_general(a, b, (((1,), (1,)), ((), ())), preferred_element_type=F32)


def _tn(a, b):
    return lax.dot_general(a, b, (((0,), (0,)), ((), ())), preferred_element_type=F32)


def _mm(a, b):
    return jnp.dot(a, b, preferred_element_type=F32)


def _mm_kernel(x_ref, w_ref, o_ref):
    o_ref[...] = _mm(x_ref[...], w_ref[...]).astype(o_ref.dtype)


def _mm_swiglu_kernel(x_ref, wg_ref, wu_ref, o_ref):
    x = x_ref[...]
    g = _mm(x, wg_ref[...])
    u = _mm(x, wu_ref[...])
    o_ref[...] = (g * _sigmoid(g) * u).astype(o_ref.dtype)


def _mm_res_kernel(x_ref, w_ref, h_ref, c_ref, o_ref, *, n_lat):
    tm = x_ref.shape[0]
    rows = pl.program_id(0) * tm + lax.broadcasted_iota(jnp.int32, (tm, 1), 0)
    coef = jnp.where(rows >= n_lat, c_ref[1:2, :], c_ref[0:1, :])
    o_ref[...] = h_ref[...] + coef * _mm(x_ref[...], w_ref[...])


_TM_PREFS = (1280, 1024, 640, 512, 256, 128, 64, 32, 16)


def _wspec(w, lead, tn, col):
    lead = tuple(lead)
    K = w.shape[-2]
    return pl.BlockSpec((None,) * len(lead) + (K, tn), lambda i, j: lead + (0, col(j)))


def matmul(x, w, lead=(), out_dtype=F32):
    M, K = x.shape
    N = w.shape[-1]
    tm = _pick(M, _TM_PREFS)
    wide = tm * K * 2 <= 11 * 1024 * 1024
    tn = _pick(N, (1024, 512, 256, 128) if wide else (512, 256, 128))
    return pl.pallas_call(
        _mm_kernel,
        grid=(M // tm, N // tn),
        in_specs=[pl.BlockSpec((tm, K), lambda i, j: (i, 0)),
                  _wspec(w, lead, tn, lambda j: j)],
        out_specs=pl.BlockSpec((tm, tn), lambda i, j: (i, j)),
        out_shape=jax.ShapeDtypeStruct((M, N), out_dtype),
        compiler_params=_cparams(("parallel", "parallel")),
    )(x, w)


def matmul_swiglu(x, w13, lead=()):
    M, K = x.shape
    F = w13.shape[-1] // 2
    tm = _pick(M, _TM_PREFS)
    tn = _pick(F, (512, 256, 128))
    nf = F // tn
    return pl.pallas_call(
        _mm_swiglu_kernel,
        grid=(M // tm, nf),
        in_specs=[pl.BlockSpec((tm, K), lambda i, j: (i, 0)),
                  _wspec(w13, lead, tn, lambda j: j),
                  _wspec(w13, lead, tn, lambda j: j + nf)],
        out_specs=pl.BlockSpec((tm, tn), lambda i, j: (i, j)),
        out_shape=jax.ShapeDtypeStruct((M, F), BF16),
        compiler_params=_cparams(("parallel", "parallel")),
    )(x, w13, w13)


def matmul_residual(x, w, h, coef, n_lat, lead=()):
    M, K = x.shape
    N = w.shape[-1]
    tm = _pick(M, _TM_PREFS)
    tn = _pick(N, (512, 256, 128))
    xmode = None
    return pl.pallas_call(
        functools.partial(_mm_res_kernel, n_lat=n_lat),
        grid=(M // tm, N // tn),
        in_specs=[pl.BlockSpec((tm, K), lambda i, j: (i, 0), pipeline_mode=xmode),
                  _wspec(w, lead, tn, lambda j: j),
                  pl.BlockSpec((tm, tn), lambda i, j: (i, j)),
                  pl.BlockSpec((SUBLANES, tn), lambda i, j: (0, j))],
        out_specs=pl.BlockSpec((tm, tn), lambda i, j: (i, j)),
        out_shape=jax.ShapeDtypeStruct((M, N), F32),
        compiler_params=_cparams(("parallel", "parallel")),
    )(x, w, h, coef)


def _normmod_kernel(h_ref, g_ref, s_ref, o_ref):
    x = h_ref[...]
    y = x * lax.rsqrt(jnp.mean(x * x, axis=-1, keepdims=True) + EPS)
    o_ref[...] = (y * g_ref[...] + s_ref[...]).astype(o_ref.dtype)


def normmod(h, g, s, n_lat, n_rows, out_dtype):
    D = h.shape[1]
    tm = _row_tile(n_lat, n_rows, (256, 128, 64, 32, 16, 8))
    seg = lambda i: (jnp.where(i * tm >= n_lat, 1, 0), 0, 0)
    return pl.pallas_call(
        _normmod_kernel,
        grid=(n_rows // tm,),
        in_specs=[pl.BlockSpec((tm, D), lambda i: (i, 0)),
                  pl.BlockSpec((None, 1, D), seg),
                  pl.BlockSpec((None, 1, D), seg)],
        out_specs=pl.BlockSpec((tm, D), lambda i: (i, 0)),
        out_shape=jax.ShapeDtypeStruct((n_rows, D), out_dtype),
        compiler_params=_cparams(("parallel",)),
    )(h, g, s)


def _flash_kernel(q_ref, k_ref, v_ref, o_ref, m_sc, acc_sc, *, tk):
    reps = tk // LANES
    m_sc[...] = jnp.full(m_sc.shape, -1e30, F32)
    acc_sc[...] = jnp.zeros(acc_sc.shape, F32)
    heads = range(G_C)
    tq = q_ref.shape[0]
    rh = min(tq, 256)
    units = [(g, r) for r in range(0, tq, rh) for g in heads]
    ahead = 4

    def kv_block(j, carry):
        off = pl.multiple_of(j * tk, tk)
        k = k_ref[pl.ds(off, tk), :]
        v = v_ref[pl.ds(off, tk), :]

        def scores(u):
            g, r = u
            return _nt(q_ref[r:r + rh, g * HEAD_C:(g + 1) * HEAD_C], k)

        def fold(u, s):
            g, r = u
            m_prev = m_sc[g, r:r + rh, :]
            m_new = jnp.maximum(m_prev, jnp.max(s, axis=-1, keepdims=True))
            p = jnp.exp2((s - jnp.concatenate([m_new] * reps, axis=1)).astype(BF16))
            alpha = jnp.exp2(m_prev - m_new)
            acc_sc[g, r:r + rh, :] = jnp.concatenate([alpha, alpha], axis=1) * acc_sc[g, r:r + rh, :] + _mm(p, v)
            m_sc[g, r:r + rh, :] = m_new

        pending = [scores(u) for u in units[:ahead]]
        for idx, u in enumerate(units):
            if idx + ahead < len(units):
                pending.append(scores(units[idx + ahead]))
            fold(u, pending[idx])
        return carry

    lax.fori_loop(0, k_ref.shape[0] // tk, kv_block, 0)
    for g in heads:
        a = acc_sc[g]
        o_ref[:, g * HEAD_C:(g + 1) * HEAD_C] = (a[:, :HEAD_C] / a[:, HEAD_C:]).astype(o_ref.dtype)


def flash_gqa(q, k, v_ext, q_rows, k_rows):
    q0, Lq = q_rows
    k0, Lk = k_rows
    assert k0 % Lk == 0
    tq = _pick(math.gcd(q0, Lq), (1024, 512, 256, 128))
    tk = _pick(Lk, (3328, 1280, 1024, 640, 512, 256, 128))
    qb, kb = q0 // tq, k0 // Lk
    gw = G_C * HEAD_C
    once = pl.Buffered(1)
    return pl.pallas_call(
        functools.partial(_flash_kernel, tk=tk),
        grid=(KV_C, Lq // tq),
        in_specs=[pl.BlockSpec((tq, gw), lambda n, i: (i + qb, n)),
                  pl.BlockSpec((Lk, HEAD_C), lambda n, i: (kb, n), pipeline_mode=once),
                  pl.BlockSpec((Lk, 2 * HEAD_C), lambda n, i: (kb, n), pipeline_mode=once)],
        out_specs=pl.BlockSpec((tq, gw), lambda n, i: (i, n)),
        out_shape=jax.ShapeDtypeStruct((Lq, H_C * HEAD_C), BF16),
        scratch_shapes=[pltpu.VMEM((G_C, tq, LANES), F32),
                        pltpu.VMEM((G_C, tq, 2 * HEAD_C), F32)],
        compiler_params=_cparams(("parallel", "arbitrary")),
    )(q, k, v_ext)


def _attn_prep_kernel(p_ref, cos_ref, sin_ref, qg_ref, kg_ref, q_ref, k_ref, v_ref):
    cos = cos_ref[...]
    sin = sin_ref[...]
    lane = lax.broadcasted_iota(jnp.int32, cos.shape, 1)
    first = (lane % (2 * ROPE_PAIRS)) < ROPE_PAIRS

    def norm_rope(x, gain):
        xn = x * lax.rsqrt(jnp.mean(x * x, axis=-1, keepdims=True) + EPS) * gain
        partner = jnp.where(first, pltpu.roll(xn, HEAD_C - ROPE_PAIRS, 1), pltpu.roll(xn, ROPE_PAIRS, 1))
        return xn * cos + partner * sin

    qg = qg_ref[...] * (LOG2E * HEAD_C ** -0.5)
    kg = kg_ref[...]
    for h in range(H_C):
        sl = slice(h * HEAD_C, (h + 1) * HEAD_C)
        q_ref[:, sl] = norm_rope(p_ref[:, sl], qg).astype(q_ref.dtype)
    ones = jnp.ones((p_ref.shape[0], HEAD_C), v_ref.dtype)
    for h in range(KV_C):
        sl = slice(h * HEAD_C, (h + 1) * HEAD_C)
        k_ref[:, sl] = norm_rope(p_ref[:, D_MODEL + h * HEAD_C:D_MODEL + (h + 1) * HEAD_C], kg).astype(k_ref.dtype)
        v_ref[:, 2 * h * HEAD_C:(2 * h + 1) * HEAD_C] = \
            p_ref[:, D_MODEL + KV_DIM + h * HEAD_C:D_MODEL + KV_DIM + (h + 1) * HEAD_C].astype(v_ref.dtype)
        v_ref[:, (2 * h + 1) * HEAD_C:(2 * h + 2) * HEAD_C] = ones


def attn_prep(p, cos, sin, q_gain, k_gain):
    L = p.shape[0]
    tm = _pick(L, (256, 128, 64, 32, 16, 8))
    row = lambda w: pl.BlockSpec((tm, w), lambda i: (i, 0))
    par = pl.BlockSpec((1, HEAD_C), lambda i: (0, 0))
    return pl.pallas_call(
        _attn_prep_kernel,
        grid=(L // tm,),
        in_specs=[row(D_MODEL + 2 * KV_DIM), row(HEAD_C), row(HEAD_C), par, par],
        out_specs=[row(D_MODEL), row(KV_DIM), row(2 * KV_DIM)],
        out_shape=[jax.ShapeDtypeStruct((L, D_MODEL), BF16),
                   jax.ShapeDtypeStruct((L, KV_DIM), BF16),
                   jax.ShapeDtypeStruct((L, 2 * KV_DIM), BF16)],
        compiler_params=_cparams(("parallel",)),
    )(p, cos, sin, q_gain, k_gain)


def _head_sum64(x):
    r_ = lax.broadcasted_iota(jnp.int32, (LANES, LANES), 0) // HEAD_A
    c_ = lax.broadcasted_iota(jnp.int32, (LANES, LANES), 1) // HEAD_A
    ones_bd = (r_ == c_).astype(BF16)
    hi = x.astype(BF16)
    lo = (x - hi.astype(F32)).astype(BF16)
    tiles = [_mm(hi[:, j:j + LANES], ones_bd) + _mm(lo[:, j:j + LANES], ones_bd) for j in range(0, x.shape[1], LANES)]
    return jnp.concatenate(tiles, axis=1)


def _ab_prep_kernel(x_ref, xp_ref, xn_ref, z_ref, zp_ref, zn_ref, mu_ref, w2_ref, w0_ref, a2_ref, a0_ref,
                    g2_ref, kk_ref, ga2_ref, gab_ref,
                    r_o, k_o, v_o, kkn_o, lw0_o, lw1_o, a0_o, a1_o, g_o, la0_o, la1_o, *, n_lat, n_rows):
    tm = x_ref.shape[0]
    i = pl.program_id(0)
    loc = lax.broadcasted_iota(jnp.int32, (tm, 1), 0)
    seg_first = (i * tm == 0) | (i * tm == n_lat)
    seg_last = ((i + 1) * tm == n_lat) | ((i + 1) * tm == n_rows)

    def prev(x, halo):
        edge = jnp.where(seg_first, 0.0, halo[SUBLANES - 1:SUBLANES, :])
        return jnp.where(loc == 0, edge, pltpu.roll(x, 1, 0))

    def nxt(x, halo):
        edge = jnp.where(seg_last, 0.0, halo[0:1, :])
        return jnp.where(loc == tm - 1, edge, pltpu.roll(x, tm - 1, 0))

    x = x_ref[...]
    mu = mu_ref[...]
    rkv = x + mu[0:1, :] * (prev(x, xp_ref[...]) - x) + mu[1:2, :] * (nxt(x, xn_ref[...]) - x)
    r_o[...] = rkv[:, :D_A]
    k = rkv[:, D_A:2 * D_A]
    k_o[...] = k
    v_o[...] = rkv[:, 2 * D_A:]
    kk = k * kk_ref[...]
    kkn_o[...] = kk / jnp.maximum(jnp.sqrt(_head_sum64(kk * kk)), 1e-12)

    def lora(off, width):
        c = z_ref[:, off:off + width]
        p_ = prev(z_ref[:, off + width:off + 2 * width], zp_ref[:, off + width:off + 2 * width])
        n_ = nxt(z_ref[:, off + 2 * width:off + 3 * width], zn_ref[:, off + 2 * width:off + 3 * width])
        return c + p_ + n_

    zw = jnp.tanh(lora(OFF_ZW, 2 * R_DECAY)).astype(BF16)
    za = lora(OFF_ZA, 2 * R_AAA).astype(BF16)
    zg = _sigmoid(lora(OFF_ZG, R_GATE)).astype(BF16)
    zga = z_ref[:, OFF_ZGA:OFF_ZGA + LANES].astype(BF16)
    g_o[...] = _mm(zg, g2_ref[...])
    for d, (lw_o, a_o, la_o) in enumerate(((lw0_o, a0_o, la0_o), (lw1_o, a1_o, la1_o))):
        lw = _mm(zw[:, d * R_DECAY:(d + 1) * R_DECAY], w2_ref[d]) + w0_ref[d:d + 1, :]
        lw_o[...] = -jnp.exp(-_softplus(-lw) - 0.5)
        a_o[...] = _sigmoid(_mm(za[:, d * R_AAA:(d + 1) * R_AAA], a2_ref[d]) + a0_ref[d:d + 1, :])
        zz = _mm(zga, ga2_ref[d]) + gab_ref[d:d + 1, :]
        la_o[...] = -_softplus(-zz) * (1.0 / GLA_GATE_NORM)


def ab_prep(p_all, P, n_lat):
    L = p_all.shape[0]
    tm = _row_tile(n_lat, L, (128, 64, 32, 16, 8))
    nb8 = tm // SUBLANES
    last8 = L // SUBLANES - 1
    wx, cz = 3 * D_A, D_AB_IN // LORA_W
    full = lambda shape: pl.BlockSpec(shape, lambda i: (0,) * len(shape))
    outw = lambda w: pl.BlockSpec((tm, w), lambda i: (i, 0))
    in_specs = [
        pl.BlockSpec((tm, wx), lambda i: (i, 0)),
        pl.BlockSpec((SUBLANES, wx), lambda i: (jnp.maximum(i * nb8 - 1, 0), 0)),
        pl.BlockSpec((SUBLANES, wx), lambda i: (jnp.minimum((i + 1) * nb8, last8), 0)),
        pl.BlockSpec((tm, LORA_W), lambda i: (i, cz)),
        pl.BlockSpec((SUBLANES, LORA_W), lambda i: (jnp.maximum(i * nb8 - 1, 0), cz)),
        pl.BlockSpec((SUBLANES, LORA_W), lambda i: (jnp.minimum((i + 1) * nb8, last8), cz)),
        full((2, wx)), full((2, R_DECAY, D_A)), full((2, D_A)), full((2, R_AAA, D_A)), full((2, D_A)),
        full((R_GATE, D_A)), full((1, D_A)), full((2, LANES, D_B_K)), full((2, D_B_K)),
    ]
    outs = [D_A] * 9 + [D_B_K] * 2
    return pl.pallas_call(
        functools.partial(_ab_prep_kernel, n_lat=n_lat, n_rows=L),
        grid=(L // tm,),
        in_specs=in_specs,
        out_specs=[outw(w) for w in outs],
        out_shape=[jax.ShapeDtypeStruct((L, w), F32) for w in outs],
        compiler_params=_cparams(("parallel",)),
    )(p_all, p_all, p_all, p_all, p_all, p_all, P['mu_rkv'], P['w2'], P['w0'], P['a2'], P['a0'],
      P['g2'], P['k_k'], P['ga2pad'], P['gab'])


def _ab_out_kernel(yf_ref, yb_ref, r_ref, k_ref, v_ref, a0_ref, a1_ref, g_ref, of_ref, ob_ref, go_ref,
                   rk_ref, ka_ref, lnw_ref, lnb_ref, gn_ref, o_ref):
    y = yf_ref[...] + yb_ref[...]
    mean = _head_sum64(y) * (1.0 / HEAD_A)
    d = y - mean
    var = _head_sum64(d * d) * (1.0 / HEAD_A)
    yn = d * lax.rsqrt(var + RWKV_LN_EPS) * lnw_ref[...] + lnb_ref[...]
    kdsum = k_ref[...] * (2.0 + (a0_ref[...] + a1_ref[...] - 2.0) * ka_ref[...])
    bonus = _head_sum64(r_ref[...] * kdsum * rk_ref[...]) * v_ref[...]
    o_ref[:, :D_A] = ((yn + bonus) * g_ref[...]).astype(o_ref.dtype)
    o = of_ref[...] + ob_ref[...]
    go = go_ref[...]
    for h in range(H_B):
        sl = slice(h * DV_B, (h + 1) * DV_B)
        oh = o[:, sl]
        yh = oh * lax.rsqrt(jnp.mean(oh * oh, axis=-1, keepdims=True) + EPS) * gn_ref[...]
        gh = go[:, sl]
        o_ref[:, D_A + h * DV_B:D_A + (h + 1) * DV_B] = (yh * gh * _sigmoid(gh)).astype(o_ref.dtype)


def ab_out(yf, yb, prep, of, ob, p_all, P, n_lat):
    L = yf.shape[0]
    tm = _row_tile(n_lat, L, (128, 64, 32, 16, 8))
    rw = pl.BlockSpec((tm, D_A), lambda i: (i, 0))
    par = lambda w: pl.BlockSpec((1, w), lambda i: (0, 0))
    go_spec = pl.BlockSpec((tm, D_B_V), lambda i: (i, (3 * D_A + 2 * D_B_K + D_B_V) // D_B_V))
    return pl.pallas_call(
        _ab_out_kernel,
        grid=(L // tm,),
        in_specs=[rw] * 10 + [go_spec, par(D_A), par(D_A), par(D_A), par(D_A), par(DV_B)],
        out_specs=pl.BlockSpec((tm, D_MODEL), lambda i: (i, 0)),
        out_shape=jax.ShapeDtypeStruct((L, D_MODEL), BF16),
        compiler_params=_cparams(("parallel",)),
    )(yf, yb, prep['r'], prep['k'], prep['v'], prep['a'][0], prep['a'][1], prep['g'], of, ob, p_all,
      P['r_k'], P['k_a'], P['ln_w'], P['ln_b'], P['g_norm'])


def _mm_split3(m, x):
    hi = x.astype(BF16)
    r1 = x - hi.astype(F32)
    mid = r1.astype(BF16)
    lo = (r1 - mid.astype(F32)).astype(BF16)
    return _mm(m, hi) + _mm(m, mid) + _mm(m, lo)


def _chunk_block(c, n_lat_chunks, n_ctx_chunks, rev):
    if rev:
        return n_lat_chunks + n_ctx_chunks - 1 - c
    return jnp.where(c < n_ctx_chunks, n_lat_chunks + c, c - n_ctx_chunks)


def _rwkv_kernel(r_ref, lw_ref, k_ref, v_ref, kk_ref, a_ref, ka_ref, y_ref, s_sc, *, npairs, rev):
    C = SCAN_CHUNK
    C2 = 2 * C

    @pl.when(pl.program_id(1) == 0)
    def _():
        s_sc[...] = jnp.zeros(s_sc.shape, F32)

    row = lax.broadcasted_iota(jnp.int32, (C, C), 0)
    col = lax.broadcasted_iota(jnp.int32, (C, C), 1)
    tri = ((col >= row) if rev else (col <= row)).astype(BF16)
    row2 = lax.broadcasted_iota(jnp.int32, (C2, C2), 0)
    col2 = lax.broadcasted_iota(jnp.int32, (C2, C2), 1)
    same = (row2 // C) == (col2 // C)
    strict2 = same & ((col2 > row2) if rev else (col2 < row2))
    incl2 = same & ((col2 >= row2) if rev else (col2 <= row2))
    eye2 = (row2 == col2).astype(F32)
    last = 0 if rev else C - 1
    head0 = lax.broadcasted_iota(jnp.int32, (C, LANES), 1) < HEAD_A

    lw = lw_ref[...]
    cl = _mm_split3(tri, lw)
    cl_last = cl[last:last + 1, :]
    kk = kk_ref[...]
    a = a_ref[...]
    k = k_ref[...] * (1.0 + (a - 1.0) * ka_ref[...])
    b = a * kk
    p_inv = jnp.exp(-cl)
    tail = jnp.exp(cl_last - cl)
    kkd = kk * jnp.exp(cl - lw)
    rd = r_ref[...] * jnp.exp(cl)
    kinv = k * p_inv
    binv = b * p_inv
    kdec = k * tail
    nbdec = -(b * tail)
    v = v_ref[...]
    decay_last = jnp.exp(cl_last)

    def stack(x, p):
        xs = x[:, p * LANES:(p + 1) * LANES]
        return jnp.concatenate([jnp.where(head0, xs, 0.0), jnp.where(head0, 0.0, xs)], axis=0).astype(BF16)

    pairs = range(npairs)
    cat0 = lambda a_, b_: jnp.concatenate([a_, b_], axis=0)
    lhs = [cat0(stack(kkd, p), stack(rd, p)) for p in pairs]
    rhs_g = [cat0(stack(kinv, p), stack(binv, p)) for p in pairs]
    dec2 = [cat0(stack(kdec, p), stack(nbdec, p)) for p in pairs]
    v2 = [stack(v, p) for p in pairs]
    G = [_nt(lhs[p], rhs_g[p]) for p in pairs]
    a_kk = [jnp.where(strict2, G[p][:C2, :C2], 0.0).astype(BF16) for p in pairs]
    n1 = [jnp.where(strict2, -G[p][:C2, C2:], 0.0).astype(BF16) for p in pairs]
    b_rkb = [jnp.concatenate([jnp.where(incl2, G[p][C2:, :C2], 0.0),
                              jnp.where(incl2, -G[p][C2:, C2:], 0.0)], axis=1).astype(BF16) for p in pairs]

    powers = [n1]
    for _ in range(5):
        prev = powers[-1]
        powers.append([_mm(prev[p], prev[p]).astype(BF16) for p in pairs])
    tinv = [eye2 + n1[p].astype(F32) for p in pairs]
    for npow in powers[1:]:
        tinv = [tinv[p] + _mm(npow[p], tinv[p].astype(BF16)) for p in pairs]
    tinv = [t.astype(BF16) for t in tinv]

    S = [s_sc[p] for p in pairs]
    inter = [_nt(lhs[p], S[p].astype(BF16)) for p in pairs]
    rhs = [inter[p][:C2] + _mm(a_kk[p], v2[p]) for p in pairs]
    uf = [_mm(tinv[p], rhs[p].astype(BF16)) for p in pairs]
    res = [rhs[p] - uf[p] + _mm(n1[p], uf[p].astype(BF16)) for p in pairs]
    u2 = [(uf[p] + _mm(tinv[p], res[p].astype(BF16))).astype(BF16) for p in pairs]
    vu = [cat0(v2[p], u2[p]) for p in pairs]
    y2 = [inter[p][C2:] + _mm(b_rkb[p], vu[p]) for p in pairs]
    for p in pairs:
        sl = slice(p * LANES, (p + 1) * LANES)
        y_ref[:, sl] = y2[p][:C] + y2[p][C:]
        s_sc[p] = S[p] * decay_last[:, sl] + _tn(vu[p], dec2[p])


def rwkv_scan(r, lw, k, v, kk, a, k_a, n_lat, rev):
    L = r.shape[0]
    C = SCAN_CHUNK
    gw = D_A
    npairs = gw // LANES
    blk = functools.partial(_chunk_block, n_lat_chunks=n_lat // C, n_ctx_chunks=(L - n_lat) // C, rev=rev)
    spec = pl.BlockSpec((C, gw), lambda g, c: (blk(c), g))
    return pl.pallas_call(
        functools.partial(_rwkv_kernel, npairs=npairs, rev=rev),
        grid=(D_A // gw, L // C),
        in_specs=[spec] * 6 + [pl.BlockSpec((1, gw), lambda g, c: (0, g))],
        out_specs=spec,
        out_shape=jax.ShapeDtypeStruct((L, D_A), F32),
        scratch_shapes=[pltpu.VMEM((npairs, LANES, LANES), F32)],
        compiler_params=_cparams(("parallel", "arbitrary")),
    )(r, lw, k, v, kk, a, k_a)


def _gla_kernel(qf_ref, kf_ref, vf_ref, gf_ref, qb_ref, kb_ref, vb_ref, gb_ref, of_ref, ob_ref, s_sc):
    C = SCAN_CHUNK

    @pl.when(pl.program_id(0) == 0)
    def _():
        s_sc[...] = jnp.zeros(s_sc.shape, F32)

    row = lax.broadcasted_iota(jnp.int32, (C, C), 0)
    col = lax.broadcasted_iota(jnp.int32, (C, C), 1)
    trow = lax.broadcasted_iota(jnp.int32, (C, 1), 0)
    eye = row == col

    def prep(rev, q_ref, k_ref, v_ref, g_ref):
        le = (col >= row) if rev else (col <= row)
        mats = [jnp.where(le, 1.0, 0.0)]
        levels = []
        n = C
        while n >= 2:
            half = n // 2
            same = (row // n) == (col // n)
            row_2nd = ((row % n) >= half) != rev
            col_2nd = ((col % n) >= half) != rev
            mats.append(jnp.where(same & row_2nd & col_2nd & le, 1.0, 0.0)
                        - jnp.where(same & (~row_2nd) & (~col_2nd) & (~le), 1.0, 0.0))
            levels.append((same, ((trow % n) >= half) != rev))
            n = half
        mall = jnp.concatenate(mats, axis=0).astype(BF16)
        call = _mm_split3(mall, g_ref[...])
        b = call[:C]
        last = 0 if rev else C - 1
        b_last = b[last:last + 1, :]
        q = q_ref[...] * (DK_B ** -0.5)
        k = k_ref[...]
        qts, kts = [], []
        for l, (_, second) in enumerate(levels):
            e = jnp.exp(-jnp.abs(call[(l + 1) * C:(l + 2) * C]))
            qts.append(jnp.where(second, q * e, 0.0).astype(BF16))
            kts.append(jnp.where(second, 0.0, k * e).astype(BF16))
        return dict(sames=[s for s, _ in levels], qts=qts, kts=kts, qk=q * k,
                    qe=(q * jnp.exp(b)).astype(BF16), kd=(k * jnp.exp(b_last - b)).astype(BF16),
                    dec=jnp.exp(b_last), vb=v_ref[...].astype(BF16))

    st = [prep(False, qf_ref, kf_ref, vf_ref, gf_ref), prep(True, qb_ref, kb_ref, vb_ref, gb_ref)]
    o_refs = (of_ref, ob_ref)
    units = [(d, h) for d in range(2) for h in range(H_B)]
    ks = lambda h: slice(h * DK_B, (h + 1) * DK_B)
    vs = lambda h: slice(h * DV_B, (h + 1) * DV_B)
    St = [s_sc[d, h] for d, h in units]
    o_inter = [_nt(st[d]['qe'][:, ks(h)], St[i].astype(BF16)) for i, (d, h) in enumerate(units)]
    att = [jnp.where(eye, jnp.sum(st[d]['qk'][:, ks(h)], axis=-1, keepdims=True), 0.0) for d, h in units]
    for l in range(len(st[0]['sames'])):
        att = [att[i] + jnp.where(st[d]['sames'][l], _nt(st[d]['qts'][l][:, ks(h)], st[d]['kts'][l][:, ks(h)]), 0.0)
               for i, (d, h) in enumerate(units)]
    for i, (d, h) in enumerate(units):
        vb = st[d]['vb'][:, vs(h)]
        o_refs[d][:, vs(h)] = o_inter[i] + _mm(att[i].astype(BF16), vb)
        s_sc[d, h] = St[i] * st[d]['dec'][:, ks(h)] + _tn(vb, st[d]['kd'][:, ks(h)])


def gla_scan(p_all, g_fwd, g_bwd, n_lat):
    L = p_all.shape[0]
    C = SCAN_CHUNK
    qc, vc = 3 * D_A // D_B_K, (3 * D_A + 2 * D_B_K) // D_B_V
    in_specs, out_specs = [], []
    for rev in (False, True):
        blk = functools.partial(_chunk_block, n_lat_chunks=n_lat // C, n_ctx_chunks=(L - n_lat) // C, rev=rev)
        in_specs += [pl.BlockSpec((C, D_B_K), lambda c, blk=blk: (blk(c), qc)),
                     pl.BlockSpec((C, D_B_K), lambda c, blk=blk: (blk(c), qc + 1)),
                     pl.BlockSpec((C, D_B_V), lambda c, blk=blk: (blk(c), vc)),
                     pl.BlockSpec((C, D_B_K), lambda c, blk=blk: (blk(c), 0))]
        out_specs.append(pl.BlockSpec((C, D_B_V), lambda c, blk=blk: (blk(c), 0)))
    return pl.pallas_call(
        _gla_kernel,
        grid=(L // C,),
        in_specs=in_specs,
        out_specs=out_specs,
        out_shape=[jax.ShapeDtypeStruct((L, D_B_V), F32)] * 2,
        scratch_shapes=[pltpu.VMEM((2, H_B, DV_B, DK_B), F32)],
        compiler_params=_cparams(("arbitrary",)),
    )(p_all, p_all, p_all, g_fwd, p_all, p_all, p_all, g_bwd)


def _seg(m, j):
    return m[0:2, j * D_MODEL:(j + 1) * D_MODEL]


def _coef(rows2):
    return jnp.zeros((SUBLANES, D_MODEL), F32).at[0:2].set(rows2)


def _ffn_half(h, m, kidx, gain, w13, w2, lead, n_lat, n_rows):
    shift, scale, gate = _seg(m, 3 * kidx), _seg(m, 3 * kidx + 1), _seg(m, 3 * kidx + 2)
    u = normmod(h, (gain[None, :] * (1.0 + scale))[:, None, :], shift[:, None, :], n_lat, n_rows, BF16)
    act = matmul_swiglu(u, w13, lead)
    return matmul_residual(act, w2, h, _coef(FFN_RES * gate), n_lat, lead)


def _shifted_lora(w, mu):
    return jnp.concatenate([(1.0 - mu[0] - mu[1])[:, None] * w, mu[0][:, None] * w, mu[1][:, None] * w], axis=1)


def _mixer_ab(u, P, n_lat):
    p_all = matmul(u, P['w_big'])
    r, k, v, kk, lw0, lw1, a0, a1, g, la0, la1 = ab_prep(p_all, P, n_lat)
    prep = dict(r=r, k=k, v=v, a=(a0, a1), g=g)
    yf = rwkv_scan(r, lw0, k, v, kk, a0, P['k_a'], n_lat, False)
    yb = rwkv_scan(r, lw1, k, v, kk, a1, P['k_a'], n_lat, True)
    of, ob = gla_scan(p_all, la0, la1, n_lat)
    return ab_out(yf, yb, prep, of, ob, p_all, P, n_lat)


def _rope_tables(n_lat, n_rows):
    rows = n_lat // GRID_W
    rr = jnp.repeat(jnp.arange(rows), GRID_W)
    cc = jnp.tile(jnp.arange(GRID_W), rows)
    inv_freq = ROPE_THETA ** (-jnp.arange(ROPE_PAIRS, dtype=F32) / ROPE_PAIRS)
    ar = rr.astype(F32)[:, None] * inv_freq
    ac = cc.astype(F32)[:, None] * inv_freq
    cos = jnp.concatenate([jnp.cos(ar), jnp.cos(ar), jnp.cos(ac), jnp.cos(ac)], axis=1)
    sin = jnp.concatenate([-jnp.sin(ar), jnp.sin(ar), -jnp.sin(ac), jnp.sin(ac)], axis=1)
    n_ctx = n_rows - n_lat
    cos = jnp.concatenate([cos, jnp.ones((n_ctx, HEAD_C), F32)], axis=0)
    sin = jnp.concatenate([sin, jnp.zeros((n_ctx, HEAD_C), F32)], axis=0)
    return cos, sin


def _mixer_c(u, P, n_lat, need_ctx):
    L = u.shape[0]
    p = matmul(u, P['w_in'], P['lead'])
    q, k, v_ext = attn_prep(p, P['cos'], P['sin'], P['q_gain'], P['k_gain'])
    o_l = flash_gqa(q, k, v_ext, (0, n_lat), (0, L))
    if not need_ctx:
        return o_l
    o_c = flash_gqa(q, k, v_ext, (n_lat, L - n_lat), (n_lat, L - n_lat))
    return jnp.concatenate([o_l, o_c], axis=0)


def kernel(x, c, ctx, c_ctx, ada_down, ada_up, ada_bias, norm_gains, final_gain, ffn_w13, ffn_w2, ab_w_in, ab_w_out, rwkv_mu_rkv, rwkv_mu_lora, rwkv_w1, rwkv_w2, rwkv_w0, rwkv_a1, rwkv_a2, rwkv_a0, rwkv_g1, rwkv_g2, rwkv_k_k, rwkv_k_a, rwkv_r_k, rwkv_ln_w, rwkv_ln_b, gla_a1, gla_a2, gla_ab, gla_norm, attn_w_in, attn_w_out, attn_q_norm, attn_k_norm):
    D = D_MODEL
    n_lat = x.shape[1]
    h = jnp.concatenate([x[0], ctx[0]], axis=0)
    n_rows = h.shape[0]
    bf = lambda t: t.astype(BF16)
    cvec = jnp.zeros((16, D), F32).at[0].set(c[0]).at[1].set(c_ctx)
    cs = bf(cvec * jax.nn.sigmoid(cvec))
    cos, sin = _rope_tables(n_lat, n_rows)
    cat1 = lambda ts: jnp.concatenate(ts, axis=1)
    w13_s, w2_s, ada_down_s, ada_up_s = bf(ffn_w13), bf(ffn_w2), bf(ada_down), bf(ada_up)
    ab_w_out_s, attn_w_in_s, attn_w_out_s = bf(ab_w_out), bf(attn_w_in), bf(attn_w_out)
    for layer in range(DEPTH):
        last = layer == DEPTH - 1
        i = layer // 2
        m = matmul(bf(matmul(cs, ada_down_s, (layer,))), ada_up_s, (layer,)) + ada_bias[layer]
        gains = norm_gains[layer]
        h = _ffn_half(h, m, 0, gains[0], w13_s, w2_s, (layer, 0), n_lat, n_rows)
        u = normmod(h, (gains[1][None] * (1.0 + _seg(m, 4)))[:, None, :], _seg(m, 3)[:, None, :], n_lat, n_rows, BF16)
        if layer % 2 == 0:
            mu = rwkv_mu_lora[i]
            rg = R_GLA_GATE
            lora = cat1([_shifted_lora(cat1([rwkv_w1[i, 0], rwkv_w1[i, 1]]), mu[0]),
                         _shifted_lora(cat1([rwkv_a1[i, 0], rwkv_a1[i, 1]]), mu[1]),
                         _shifted_lora(rwkv_g1[i], mu[2]),
                         gla_a1[i, 0], gla_a1[i, 1]])
            lora = jnp.pad(lora, ((0, 0), (0, LORA_W - lora.shape[1])))
            ga2pad = jnp.stack([jnp.pad(gla_a2[i, d], ((d * rg, LANES - (d + 1) * rg), (0, 0))) for d in range(2)])
            P = dict(
                w_big=bf(cat1([ab_w_in[i], lora])), mu_rkv=rwkv_mu_rkv[i],
                w2=bf(rwkv_w2[i]), w0=rwkv_w0[i], a2=bf(rwkv_a2[i]), a0=rwkv_a0[i], g2=bf(rwkv_g2[i]),
                k_k=rwkv_k_k[i][None], k_a=rwkv_k_a[i][None], r_k=rwkv_r_k[i].reshape(1, D_A),
                ln_w=rwkv_ln_w[i][None], ln_b=rwkv_ln_b[i][None],
                ga2pad=bf(ga2pad), gab=gla_ab[i], g_norm=gla_norm[i][None])
            y = _mixer_ab(u, P, n_lat)
            w_out = ab_w_out_s
        else:
            P = dict(w_in=attn_w_in_s, lead=(i,), q_gain=attn_q_norm[i][None], k_gain=attn_k_norm[i][None],
                     cos=cos, sin=sin)
            y = _mixer_c(u, P, n_lat, not last)
            w_out = attn_w_out_s
        if last:
            n_rows = n_lat
        h = matmul_residual(y, w_out, h, _coef(_seg(m, 5)), n_lat, (i,))
        h = _ffn_half(h, m, 2, gains[2], w13_s, w2_s, (layer, 1), n_lat, n_rows)
    ones2 = jnp.stack([final_gain, final_gain])[:, None, :]
    out = normmod(h, ones2, jnp.zeros((2, 1, D), F32), n_lat, n_lat, F32)
    return out[None]
```

```python
import functools
import math

import jax
import jax.numpy as jnp
from jax import lax
from jax.experimental import pallas as pl
from jax.experimental.pallas import tpu as pltpu

F32 = jnp.float32
BF16 = jnp.bfloat16

D_MODEL = 4096
DEPTH = 4
GRID_W = 64
N_MOD = 9
D_FF = 6144
FFN_RES = 0.5
EPS = 1e-6
D_A = D_MODEL // 2
HEAD_A = 64
H_A = D_A // HEAD_A
R_DECAY = 128
R_AAA = 128
R_GATE = 256
RWKV_LN_EPS = 64e-5
D_B_K = D_MODEL // 4
D_B_V = D_MODEL // 2
H_B = 4
DK_B = D_B_K // H_B
DV_B = D_B_V // H_B
R_GLA_GATE = 16
GLA_GATE_NORM = 16.0
D_AB_IN = 3 * D_A + 2 * D_B_K + 2 * D_B_V
HEAD_C = 128
H_C = D_MODEL // HEAD_C
KV_C = 8
G_C = H_C // KV_C
KV_DIM = KV_C * HEAD_C
ROPE_PAIRS = HEAD_C // 4
ROPE_THETA = 10000.0
LOG2E = 1.4426950408889634

LANES = 128
SUBLANES = 8
SCAN_CHUNK = 64
VMEM_LIMIT = 60 * 1024 * 1024

LORA_W = 3072
AB_COLS = D_AB_IN + LORA_W
OFF_ZW, OFF_ZA, OFF_ZG = 0, 3 * 2 * R_DECAY, 3 * 2 * R_DECAY + 3 * 2 * R_AAA
OFF_ZGA = OFF_ZG + 3 * R_GATE


def _pick(n, prefs):
    for p in prefs:
        if n % p == 0:
            return p
    return n


def _cparams(sem):
    return pltpu.CompilerParams(dimension_semantics=sem, vmem_limit_bytes=VMEM_LIMIT)


def _row_tile(n_lat, n_rows, prefs):
    return _pick(math.gcd(n_lat, n_rows - n_lat) if n_rows > n_lat else n_lat, prefs)


def _sigmoid(x):
    return 1.0 / (1.0 + jnp.exp(-x))


def _softplus(x):
    return jnp.maximum(x, 0.0) + jnp.log(1.0 + jnp.exp(-jnp.abs(x)))


def _nt(a, b):
    return lax.dot_general(a, b, (((1,), (1,)), ((), ())), preferred_element_type=F32)


def _tn(a, b):
    return lax.dot_general(a, b, (((0,), (0,)), ((), ())), preferred_element_type=F32)


def _mm(a, b):
    return jnp.dot(a, b, preferred_element_type=F32)


def _mm_kernel(x_ref, w_ref, o_ref):
    o_ref[...] = _mm(x_ref[...], w_ref[...]).astype(o_ref.dtype)


def _mm_swiglu_kernel(x_ref, wg_ref, wu_ref, o_ref):
    x = x_ref[...]
    g = _mm(x, wg_ref[...])
    u = _mm(x, wu_ref[...])
    o_ref[...] = (g * _sigmoid(g) * u).astype(o_ref.dtype)


def _mm_res_kernel(x_ref, w_ref, h_ref, c_ref, o_ref, *, n_lat):
    tm = x_ref.shape[0]
    rows = pl.program_id(0) * tm + lax.broadcasted_iota(jnp.int32, (tm, 1), 0)
    coef = jnp.where(rows >= n_lat, c_ref[1:2, :], c_ref[0:1, :])
    o_ref[...] = h_ref[...] + coef * _mm(x_ref[...], w_ref[...])


_TM_PREFS = (1280, 1024, 640, 512, 256, 128, 64, 32, 16)


def _wspec(w, lead, tn, col):
    lead = tuple(lead)
    K = w.shape[-2]
    return pl.BlockSpec((None,) * len(lead) + (K, tn), lambda i, j: lead + (0, col(j)))


def matmul(x, w, lead=(), out_dtype=F32):
    M, K = x.shape
    N = w.shape[-1]
    tm = _pick(M, _TM_PREFS)
    wide = tm * K * 2 <= 11 * 1024 * 1024
    tn = _pick(N, (1024, 512, 256, 128) if wide else (512, 256, 128))
    return pl.pallas_call(
        _mm_kernel,
        grid=(M // tm, N // tn),
        in_specs=[pl.BlockSpec((tm, K), lambda i, j: (i, 0)),
                  _wspec(w, lead, tn, lambda j: j)],
        out_specs=pl.BlockSpec((tm, tn), lambda i, j: (i, j)),
        out_shape=jax.ShapeDtypeStruct((M, N), out_dtype),
        compiler_params=_cparams(("parallel", "parallel")),
    )(x, w)


def matmul_swiglu(x, w13, lead=()):
    M, K = x.shape
    F = w13.shape[-1] // 2
    tm = _pick(M, _TM_PREFS)
    tn = _pick(F, (512, 256, 128))
    nf = F // tn
    return pl.pallas_call(
        _mm_swiglu_kernel,
        grid=(M // tm, nf),
        in_specs=[pl.BlockSpec((tm, K), lambda i, j: (i, 0)),
                  _wspec(w13, lead, tn, lambda j: j),
                  _wspec(w13, lead, tn, lambda j: j + nf)],
        out_specs=pl.BlockSpec((tm, tn), lambda i, j: (i, j)),
        out_shape=jax.ShapeDtypeStruct((M, F), BF16),
        compiler_params=_cparams(("parallel", "parallel")),
    )(x, w13, w13)


def matmul_residual(x, w, h, coef, n_lat, lead=()):
    M, K = x.shape
    N = w.shape[-1]
    tm = _pick(M, _TM_PREFS)
    tn = _pick(N, (512, 256, 128))
    xmode = None
    return pl.pallas_call(
        functools.partial(_mm_res_kernel, n_lat=n_lat),
        grid=(M // tm, N // tn),
        in_specs=[pl.BlockSpec((tm, K), lambda i, j: (i, 0), pipeline_mode=xmode),
                  _wspec(w, lead, tn, lambda j: j),
                  pl.BlockSpec((tm, tn), lambda i, j: (i, j)),
                  pl.BlockSpec((SUBLANES, tn), lambda i, j: (0, j))],
        out_specs=pl.BlockSpec((tm, tn), lambda i, j: (i, j)),
        out_shape=jax.ShapeDtypeStruct((M, N), F32),
        compiler_params=_cparams(("parallel", "parallel")),
    )(x, w, h, coef)


def _normmod_kernel(h_ref, g_ref, s_ref, o_ref):
    x = h_ref[...]
    y = x * lax.rsqrt(jnp.mean(x * x, axis=-1, keepdims=True) + EPS)
    o_ref[...] = (y * g_ref[...] + s_ref[...]).astype(o_ref.dtype)


def normmod(h, g, s, n_lat, n_rows, out_dtype):
    D = h.shape[1]
    tm = _row_tile(n_lat, n_rows, (256, 128, 64, 32, 16, 8))
    seg = lambda i: (jnp.where(i * tm >= n_lat, 1, 0), 0, 0)
    return pl.pallas_call(
        _normmod_kernel,
        grid=(n_rows // tm,),
        in_specs=[pl.BlockSpec((tm, D), lambda i: (i, 0)),
                  pl.BlockSpec((None, 1, D), seg),
                  pl.BlockSpec((None, 1, D), seg)],
        out_specs=pl.BlockSpec((tm, D), lambda i: (i, 0)),
        out_shape=jax.ShapeDtypeStruct((n_rows, D), out_dtype),
        compiler_params=_cparams(("parallel",)),
    )(h, g, s)


def _flash_kernel(q_ref, k_ref, v_ref, o_ref, m_sc, acc_sc, *, tk):
    reps = tk // LANES
    m_sc[...] = jnp.full(m_sc.shape, -1e30, F32)
    acc_sc[...] = jnp.zeros(acc_sc.shape, F32)
    heads = range(G_C)
    tq = q_ref.shape[0]
    rh = min(tq, 256)
    units = [(g, r) for r in range(0, tq, rh) for g in heads]
    ahead = 4

    def kv_block(j, carry):
        off = pl.multiple_of(j * tk, tk)
        k = k_ref[pl.ds(off, tk), :]
        v = v_ref[pl.ds(off, tk), :]

        def scores(u):
            g, r = u
            return _nt(q_ref[r:r + rh, g * HEAD_C:(g + 1) * HEAD_C], k)

        def fold(u, s):
            g, r = u
            m_prev = m_sc[g, r:r + rh, :]
            m_new = jnp.maximum(m_prev, jnp.max(s, axis=-1, keepdims=True))
            p = jnp.exp2((s - jnp.concatenate([m_new] * reps, axis=1)).astype(BF16))
            alpha = jnp.exp2(m_prev - m_new)
            acc_sc[g, r:r + rh, :] = jnp.concatenate([alpha, alpha], axis=1) * acc_sc[g, r:r + rh, :] + _mm(p, v)
            m_sc[g, r:r + rh, :] = m_new

        pending = [scores(u) for u in units[:ahead]]
        for idx, u in enumerate(units):
            if idx + ahead < len(units):
                pending.append(scores(units[idx + ahead]))
            fold(u, pending[idx])
        return carry

    lax.fori_loop(0, k_ref.shape[0] // tk, kv_block, 0)
    for g in heads:
        a = acc_sc[g]
        o_ref[:, g * HEAD_C:(g + 1) * HEAD_C] = (a[:, :HEAD_C] / a[:, HEAD_C:]).astype(o_ref.dtype)


def flash_gqa(q, k, v_ext, q_rows, k_rows):
    q0, Lq = q_rows
    k0, Lk = k_rows
    assert k0 % Lk == 0
    tq = _pick(math.gcd(q0, Lq), (1024, 512, 256, 128))
    tk = _pick(Lk, (3328, 1280, 1024, 640, 512, 256, 128))
    qb, kb = q0 // tq, k0 // Lk
    gw = G_C * HEAD_C
    once = pl.Buffered(1)
    return pl.pallas_call(
        functools.partial(_flash_kernel, tk=tk),
        grid=(KV_C, Lq // tq),
        in_specs=[pl.BlockSpec((tq, gw), lambda n, i: (i + qb, n)),
                  pl.BlockSpec((Lk, HEAD_C), lambda n, i: (kb, n), pipeline_mode=once),
                  pl.BlockSpec((Lk, 2 * HEAD_C), lambda n, i: (kb, n), pipeline_mode=once)],
        out_specs=pl.BlockSpec((tq, gw), lambda n, i: (i, n)),
        out_shape=jax.ShapeDtypeStruct((Lq, H_C * HEAD_C), BF16),
        scratch_shapes=[pltpu.VMEM((G_C, tq, LANES), F32),
                        pltpu.VMEM((G_C, tq, 2 * HEAD_C), F32)],
        compiler_params=_cparams(("parallel", "arbitrary")),
    )(q, k, v_ext)


def _attn_prep_kernel(p_ref, cos_ref, sin_ref, qg_ref, kg_ref, q_ref, k_ref, v_ref):
    cos = cos_ref[...]
    sin = sin_ref[...]
    lane = lax.broadcasted_iota(jnp.int32, cos.shape, 1)
    first = (lane % (2 * ROPE_PAIRS)) < ROPE_PAIRS

    def norm_rope(x, gain):
        xn = x * lax.rsqrt(jnp.mean(x * x, axis=-1, keepdims=True) + EPS) * gain
        partner = jnp.where(first, pltpu.roll(xn, HEAD_C - ROPE_PAIRS, 1), pltpu.roll(xn, ROPE_PAIRS, 1))
        return xn * cos + partner * sin

    qg = qg_ref[...] * (LOG2E * HEAD_C ** -0.5)
    kg = kg_ref[...]
    for h in range(H_C):
        sl = slice(h * HEAD_C, (h + 1) * HEAD_C)
        q_ref[:, sl] = norm_rope(p_ref[:, sl], qg).astype(q_ref.dtype)
    ones = jnp.ones((p_ref.shape[0], HEAD_C), v_ref.dtype)
    for h in range(KV_C):
        sl = slice(h * HEAD_C, (h + 1) * HEAD_C)
        k_ref[:, sl] = norm_rope(p_ref[:, D_MODEL + h * HEAD_C:D_MODEL + (h + 1) * HEAD_C], kg).astype(k_ref.dtype)
        v_ref[:, 2 * h * HEAD_C:(2 * h + 1) * HEAD_C] = \
            p_ref[:, D_MODEL + KV_DIM + h * HEAD_C:D_MODEL + KV_DIM + (h + 1) * HEAD_C].astype(v_ref.dtype)
        v_ref[:, (2 * h + 1) * HEAD_C:(2 * h + 2) * HEAD_C] = ones


def attn_prep(p, cos, sin, q_gain, k_gain):
    L = p.shape[0]
    tm = _pick(L, (256, 128, 64, 32, 16, 8))
    row = lambda w: pl.BlockSpec((tm, w), lambda i: (i, 0))
    par = pl.BlockSpec((1, HEAD_C), lambda i: (0, 0))
    return pl.pallas_call(
        _attn_prep_kernel,
        grid=(L // tm,),
        in_specs=[row(D_MODEL + 2 * KV_DIM), row(HEAD_C), row(HEAD_C), par, par],
        out_specs=[row(D_MODEL), row(KV_DIM), row(2 * KV_DIM)],
        out_shape=[jax.ShapeDtypeStruct((L, D_MODEL), BF16),
                   jax.ShapeDtypeStruct((L, KV_DIM), BF16),
                   jax.ShapeDtypeStruct((L, 2 * KV_DIM), BF16)],
        compiler_params=_cparams(("parallel",)),
    )(p, cos, sin, q_gain, k_gain)


def _head_sum64(x):
    r_ = lax.broadcasted_iota(jnp.int32, (LANES, LANES), 0) // HEAD_A
    c_ = lax.broadcasted_iota(jnp.int32, (LANES, LANES), 1) // HEAD_A
    ones_bd = (r_ == c_).astype(BF16)
    hi = x.astype(BF16)
    lo = (x - hi.astype(F32)).astype(BF16)
    tiles = [_mm(hi[:, j:j + LANES], ones_bd) + _mm(lo[:, j:j + LANES], ones_bd) for j in range(0, x.shape[1], LANES)]
    return jnp.concatenate(tiles, axis=1)


def _ab_prep_kernel(x_ref, xp_ref, xn_ref, z_ref, zp_ref, zn_ref, mu_ref, w2_ref, w0_ref, a2_ref, a0_ref,
                    g2_ref, kk_ref, ga2_ref, gab_ref,
                    r_o, k_o, v_o, kkn_o, lw0_o, lw1_o, a0_o, a1_o, g_o, la0_o, la1_o, *, n_lat, n_rows):
    tm = x_ref.shape[0]
    i = pl.program_id(0)
    loc = lax.broadcasted_iota(jnp.int32, (tm, 1), 0)
    seg_first = (i * tm == 0) | (i * tm == n_lat)
    seg_last = ((i + 1) * tm == n_lat) | ((i + 1) * tm == n_rows)

    def prev(x, halo):
        edge = jnp.where(seg_first, 0.0, halo[SUBLANES - 1:SUBLANES, :])
        return jnp.where(loc == 0, edge, pltpu.roll(x, 1, 0))

    def nxt(x, halo):
        edge = jnp.where(seg_last, 0.0, halo[0:1, :])
        return jnp.where(loc == tm - 1, edge, pltpu.roll(x, tm - 1, 0))

    x = x_ref[...]
    mu = mu_ref[...]
    rkv = x + mu[0:1, :] * (prev(x, xp_ref[...]) - x) + mu[1:2, :] * (nxt(x, xn_ref[...]) - x)
    r_o[...] = rkv[:, :D_A]
    k = rkv[:, D_A:2 * D_A]
    k_o[...] = k
    v_o[...] = rkv[:, 2 * D_A:]
    kk = k * kk_ref[...]
    kkn_o[...] = kk / jnp.maximum(jnp.sqrt(_head_sum64(kk * kk)), 1e-12)

    def lora(off, width):
        c = z_ref[:, off:off + width]
        p_ = prev(z_ref[:, off + width:off + 2 * width], zp_ref[:, off + width:off + 2 * width])
        n_ = nxt(z_ref[:, off + 2 * width:off + 3 * width], zn_ref[:, off + 2 * width:off + 3 * width])
        return c + p_ + n_

    zw = jnp.tanh(lora(OFF_ZW, 2 * R_DECAY)).astype(BF16)
    za = lora(OFF_ZA, 2 * R_AAA).astype(BF16)
    zg = _sigmoid(lora(OFF_ZG, R_GATE)).astype(BF16)
    zga = z_ref[:, OFF_ZGA:OFF_ZGA + LANES].astype(BF16)
    g_o[...] = _mm(zg, g2_ref[...])
    for d, (lw_o, a_o, la_o) in enumerate(((lw0_o, a0_o, la0_o), (lw1_o, a1_o, la1_o))):
        lw = _mm(zw[:, d * R_DECAY:(d + 1) * R_DECAY], w2_ref[d]) + w0_ref[d:d + 1, :]
        lw_o[...] = -jnp.exp(-_softplus(-lw) - 0.5)
        a_o[...] = _sigmoid(_mm(za[:, d * R_AAA:(d + 1) * R_AAA], a2_ref[d]) + a0_ref[d:d + 1, :])
        zz = _mm(zga, ga2_ref[d]) + gab_ref[d:d + 1, :]
        la_o[...] = -_softplus(-zz) * (1.0 / GLA_GATE_NORM)


def ab_prep(p_all, P, n_lat):
    L = p_all.shape[0]
    tm = _row_tile(n_lat, L, (128, 64, 32, 16, 8))
    nb8 = tm // SUBLANES
    last8 = L // SUBLANES - 1
    wx, cz = 3 * D_A, D_AB_IN // LORA_W
    full = lambda shape: pl.BlockSpec(shape, lambda i: (0,) * len(shape))
    outw = lambda w: pl.BlockSpec((tm, w), lambda i: (i, 0))
    in_specs = [
        pl.BlockSpec((tm, wx), lambda i: (i, 0)),
        pl.BlockSpec((SUBLANES, wx), lambda i: (jnp.maximum(i * nb8 - 1, 0), 0)),
        pl.BlockSpec((SUBLANES, wx), lambda i: (jnp.minimum((i + 1) * nb8, last8), 0)),
        pl.BlockSpec((tm, LORA_W), lambda i: (i, cz)),
        pl.BlockSpec((SUBLANES, LORA_W), lambda i: (jnp.maximum(i * nb8 - 1, 0), cz)),
        pl.BlockSpec((SUBLANES, LORA_W), lambda i: (jnp.minimum((i + 1) * nb8, last8), cz)),
        full((2, wx)), full((2, R_DECAY, D_A)), full((2, D_A)), full((2, R_AAA, D_A)), full((2, D_A)),
        full((R_GATE, D_A)), full((1, D_A)), full((2, LANES, D_B_K)), full((2, D_B_K)),
    ]
    outs = [D_A] * 9 + [D_B_K] * 2
    return pl.pallas_call(
        functools.partial(_ab_prep_kernel, n_lat=n_lat, n_rows=L),
        grid=(L // tm,),
        in_specs=in_specs,
        out_specs=[outw(w) for w in outs],
        out_shape=[jax.ShapeDtypeStruct((L, w), F32) for w in outs],
        compiler_params=_cparams(("parallel",)),
    )(p_all, p_all, p_all, p_all, p_all, p_all, P['mu_rkv'], P['w2'], P['w0'], P['a2'], P['a0'],
      P['g2'], P['k_k'], P['ga2pad'], P['gab'])


def _ab_out_kernel(yf_ref, yb_ref, r_ref, k_ref, v_ref, a0_ref, a1_ref, g_ref, of_ref, ob_ref, go_ref,
                   rk_ref, ka_ref, lnw_ref, lnb_ref, gn_ref, o_ref):
    y = yf_ref[...] + yb_ref[...]
    mean = _head_sum64(y) * (1.0 / HEAD_A)
    d = y - mean
    var = _head_sum64(d * d) * (1.0 / HEAD_A)
    yn = d * lax.rsqrt(var + RWKV_LN_EPS) * lnw_ref[...] + lnb_ref[...]
    kdsum = k_ref[...] * (2.0 + (a0_ref[...] + a1_ref[...] - 2.0) * ka_ref[...])
    bonus = _head_sum64(r_ref[...] * kdsum * rk_ref[...]) * v_ref[...]
    o_ref[:, :D_A] = ((yn + bonus) * g_ref[...]).astype(o_ref.dtype)
    o = of_ref[...] + ob_ref[...]
    go = go_ref[...]
    for h in range(H_B):
        sl = slice(h * DV_B, (h + 1) * DV_B)
        oh = o[:, sl]
        yh = oh * lax.rsqrt(jnp.mean(oh * oh, axis=-1, keepdims=True) + EPS) * gn_ref[...]
        gh = go[:, sl]
        o_ref[:, D_A + h * DV_B:D_A + (h + 1) * DV_B] = (yh * gh * _sigmoid(gh)).astype(o_ref.dtype)


def ab_out(yf, yb, prep, of, ob, p_all, P, n_lat):
    L = yf.shape[0]
    tm = _row_tile(n_lat, L, (128, 64, 32, 16, 8))
    rw = pl.BlockSpec((tm, D_A), lambda i: (i, 0))
    par = lambda w: pl.BlockSpec((1, w), lambda i: (0, 0))
    go_spec = pl.BlockSpec((tm, D_B_V), lambda i: (i, (3 * D_A + 2 * D_B_K + D_B_V) // D_B_V))
    return pl.pallas_call(
        _ab_out_kernel,
        grid=(L // tm,),
        in_specs=[rw] * 10 + [go_spec, par(D_A), par(D_A), par(D_A), par(D_A), par(DV_B)],
        out_specs=pl.BlockSpec((tm, D_MODEL), lambda i: (i, 0)),
        out_shape=jax.ShapeDtypeStruct((L, D_MODEL), BF16),
        compiler_params=_cparams(("parallel",)),
    )(yf, yb, prep['r'], prep['k'], prep['v'], prep['a'][0], prep['a'][1], prep['g'], of, ob, p_all,
      P['r_k'], P['k_a'], P['ln_w'], P['ln_b'], P['g_norm'])


def _mm_split3(m, x):
    hi = x.astype(BF16)
    r1 = x - hi.astype(F32)
    mid = r1.astype(BF16)
    lo = (r1 - mid.astype(F32)).astype(BF16)
    return _mm(m, hi) + _mm(m, mid) + _mm(m, lo)


def _chunk_block(c, n_lat_chunks, n_ctx_chunks, rev):
    if rev:
        return n_lat_chunks + n_ctx_chunks - 1 - c
    return jnp.where(c < n_ctx_chunks, n_lat_chunks + c, c - n_ctx_chunks)


def _rwkv_kernel(rf_ref, lwf_ref, kf_ref, vf_ref, kkf_ref, af_ref, rb_ref, lwb_ref, kb_ref, vb_ref, kkb_ref, ab_ref,
                 ka_ref, yf_ref, yb_ref, s_sc, *, npairs):
    @pl.when(pl.program_id(0) == 0)
    def _():
        s_sc[...] = jnp.zeros(s_sc.shape, F32)

    _rwkv_chunk(rf_ref, lwf_ref, kf_ref, vf_ref, kkf_ref, af_ref, ka_ref, yf_ref, s_sc, 0, npairs, False)
    _rwkv_chunk(rb_ref, lwb_ref, kb_ref, vb_ref, kkb_ref, ab_ref, ka_ref, yb_ref, s_sc, 1, npairs, True)


def _rwkv_chunk(r_ref, lw_ref, k_ref, v_ref, kk_ref, a_ref, ka_ref, y_ref, s_sc, d, npairs, rev):
    C = SCAN_CHUNK
    C2 = 2 * C

    row = lax.broadcasted_iota(jnp.int32, (C, C), 0)
    col = lax.broadcasted_iota(jnp.int32, (C, C), 1)
    tri = ((col >= row) if rev else (col <= row)).astype(BF16)
    row2 = lax.broadcasted_iota(jnp.int32, (C2, C2), 0)
    col2 = lax.broadcasted_iota(jnp.int32, (C2, C2), 1)
    same = (row2 // C) == (col2 // C)
    strict2 = same & ((col2 > row2) if rev else (col2 < row2))
    incl2 = same & ((col2 >= row2) if rev else (col2 <= row2))
    eye2 = (row2 == col2).astype(F32)
    last = 0 if rev else C - 1
    head0 = lax.broadcasted_iota(jnp.int32, (C, LANES), 1) < HEAD_A

    lw = lw_ref[...]
    cl = _mm_split3(tri, lw)
    cl_last = cl[last:last + 1, :]
    kk = kk_ref[...]
    a = a_ref[...]
    k = k_ref[...] * (1.0 + (a - 1.0) * ka_ref[...])
    b = a * kk
    p_inv = jnp.exp(-cl)
    tail = jnp.exp(cl_last - cl)
    kkd = kk * jnp.exp(cl - lw)
    rd = r_ref[...] * jnp.exp(cl)
    kinv = k * p_inv
    binv = b * p_inv
    kdec = k * tail
    nbdec = -(b * tail)
    v = v_ref[...]
    decay_last = jnp.exp(cl_last)

    def stack(x, p):
        xs = x[:, p * LANES:(p + 1) * LANES]
        return jnp.concatenate([jnp.where(head0, xs, 0.0), jnp.where(head0, 0.0, xs)], axis=0).astype(BF16)

    pairs = range(npairs)
    cat0 = lambda a_, b_: jnp.concatenate([a_, b_], axis=0)
    lhs = [cat0(stack(kkd, p), stack(rd, p)) for p in pairs]
    rhs_g = [cat0(stack(kinv, p), stack(binv, p)) for p in pairs]
    dec2 = [cat0(stack(kdec, p), stack(nbdec, p)) for p in pairs]
    v2 = [stack(v, p) for p in pairs]
    G = [_nt(lhs[p], rhs_g[p]) for p in pairs]
    a_kk = [jnp.where(strict2, G[p][:C2, :C2], 0.0).astype(BF16) for p in pairs]
    n1 = [jnp.where(strict2, -G[p][:C2, C2:], 0.0).astype(BF16) for p in pairs]
    b_rkb = [jnp.concatenate([jnp.where(incl2, G[p][C2:, :C2], 0.0),
                              jnp.where(incl2, -G[p][C2:, C2:], 0.0)], axis=1).astype(BF16) for p in pairs]

    powers = [n1]
    for _ in range(5):
        prev = powers[-1]
        powers.append([_mm(prev[p], prev[p]).astype(BF16) for p in pairs])
    tinv = [eye2 + n1[p].astype(F32) for p in pairs]
    for npow in powers[1:]:
        tinv = [tinv[p] + _mm(npow[p], tinv[p].astype(BF16)) for p in pairs]
    tinv = [t.astype(BF16) for t in tinv]

    S = [s_sc[d, p] for p in pairs]
    inter = [_nt(lhs[p], S[p].astype(BF16)) for p in pairs]
    rhs = [inter[p][:C2] + _mm(a_kk[p], v2[p]) for p in pairs]
    uf = [_mm(tinv[p], rhs[p].astype(BF16)) for p in pairs]
    res = [rhs[p] - uf[p] + _mm(n1[p], uf[p].astype(BF16)) for p in pairs]
    u2 = [(uf[p] + _mm(tinv[p], res[p].astype(BF16))).astype(BF16) for p in pairs]
    vu = [cat0(v2[p], u2[p]) for p in pairs]
    y2 = [inter[p][C2:] + _mm(b_rkb[p], vu[p]) for p in pairs]
    for p in pairs:
        sl = slice(p * LANES, (p + 1) * LANES)
        y_ref[:, sl] = y2[p][:C] + y2[p][C:]
        s_sc[d, p] = S[p] * decay_last[:, sl] + _tn(vu[p], dec2[p])


def rwkv_scan(r, lw_fb, k, v, kk, a_fb, k_a, n_lat):
    L = r.shape[0]
    C = SCAN_CHUNK
    npairs = D_A // LANES
    specs = []
    for rev in (False, True):
        blk = functools.partial(_chunk_block, n_lat_chunks=n_lat // C, n_ctx_chunks=(L - n_lat) // C, rev=rev)
        specs.append(pl.BlockSpec((C, D_A), lambda c, blk=blk: (blk(c), 0)))
    return pl.pallas_call(
        functools.partial(_rwkv_kernel, npairs=npairs),
        grid=(L // C,),
        in_specs=[specs[0]] * 6 + [specs[1]] * 6 + [pl.BlockSpec((1, D_A), lambda c: (0, 0))],
        out_specs=specs,
        out_shape=[jax.ShapeDtypeStruct((L, D_A), F32)] * 2,
        scratch_shapes=[pltpu.VMEM((2, npairs, LANES, LANES), F32)],
        compiler_params=_cparams(("arbitrary",)),
    )(r, lw_fb[0], k, v, kk, a_fb[0], r, lw_fb[1], k, v, kk, a_fb[1], k_a)


def _gla_kernel(qf_ref, kf_ref, vf_ref, gf_ref, qb_ref, kb_ref, vb_ref, gb_ref, of_ref, ob_ref, s_sc):
    C = SCAN_CHUNK

    @pl.when(pl.program_id(0) == 0)
    def _():
        s_sc[...] = jnp.zeros(s_sc.shape, F32)

    row = lax.broadcasted_iota(jnp.int32, (C, C), 0)
    col = lax.broadcasted_iota(jnp.int32, (C, C), 1)
    trow = lax.broadcasted_iota(jnp.int32, (C, 1), 0)
    eye = row == col

    def prep(rev, q_ref, k_ref, v_ref, g_ref):
        le = (col >= row) if rev else (col <= row)
        mats = [jnp.where(le, 1.0, 0.0)]
        levels = []
        n = C
        while n >= 2:
            half = n // 2
            same = (row // n) == (col // n)
            row_2nd = ((row % n) >= half) != rev
            col_2nd = ((col % n) >= half) != rev
            mats.append(jnp.where(same & row_2nd & col_2nd & le, 1.0, 0.0)
                        - jnp.where(same & (~row_2nd) & (~col_2nd) & (~le), 1.0, 0.0))
            levels.append((same, ((trow % n) >= half) != rev))
            n = half
        mall = jnp.concatenate(mats, axis=0).astype(BF16)
        call = _mm_split3(mall, g_ref[...])
        b = call[:C]
        last = 0 if rev else C - 1
        b_last = b[last:last + 1, :]
        q = q_ref[...] * (DK_B ** -0.5)
        k = k_ref[...]
        qts, kts = [], []
        for l, (_, second) in enumerate(levels):
            e = jnp.exp(-jnp.abs(call[(l + 1) * C:(l + 2) * C]))
            qts.append(jnp.where(second, q * e, 0.0).astype(BF16))
            kts.append(jnp.where(second, 0.0, k * e).astype(BF16))
        return dict(sames=[s for s, _ in levels], qts=qts, kts=kts, qk=q * k,
                    qe=(q * jnp.exp(b)).astype(BF16), kd=(k * jnp.exp(b_last - b)).astype(BF16),
                    dec=jnp.exp(b_last), vb=v_ref[...].astype(BF16))

    st = [prep(False, qf_ref, kf_ref, vf_ref, gf_ref), prep(True, qb_ref, kb_ref, vb_ref, gb_ref)]
    o_refs = (of_ref, ob_ref)
    units = [(d, h) for d in range(2) for h in range(H_B)]
    ks = lambda h: slice(h * DK_B, (h + 1) * DK_B)
    vs = lambda h: slice(h * DV_B, (h + 1) * DV_B)
    St = [s_sc[d, h] for d, h in units]
    o_inter = [_nt(st[d]['qe'][:, ks(h)], St[i].astype(BF16)) for i, (d, h) in enumerate(units)]
    att = [jnp.where(eye, jnp.sum(st[d]['qk'][:, ks(h)], axis=-1, keepdims=True), 0.0) for d, h in units]
    for l in range(len(st[0]['sames'])):
        att = [att[i] + jnp.where(st[d]['sames'][l], _nt(st[d]['qts'][l][:, ks(h)], st[d]['kts'][l][:, ks(h)]), 0.0)
               for i, (d, h) in enumerate(units)]
    for i, (d, h) in enumerate(units):
        vb = st[d]['vb'][:, vs(h)]
        o_refs[d][:, vs(h)] = o_inter[i] + _mm(att[i].astype(BF16), vb)
        s_sc[d, h] = St[i] * st[d]['dec'][:, ks(h)] + _tn(vb, st[d]['kd'][:, ks(h)])


def gla_scan(p_all, g_fwd, g_bwd, n_lat):
    L = p_all.shape[0]
    C = SCAN_CHUNK
    qc, vc = 3 * D_A // D_B_K, (3 * D_A + 2 * D_B_K) // D_B_V
    in_specs, out_specs = [], []
    for rev in (False, True):
        blk = functools.partial(_chunk_block, n_lat_chunks=n_lat // C, n_ctx_chunks=(L - n_lat) // C, rev=rev)
        in_specs += [pl.BlockSpec((C, D_B_K), lambda c, blk=blk: (blk(c), qc)),
                     pl.BlockSpec((C, D_B_K), lambda c, blk=blk: (blk(c), qc + 1)),
                     pl.BlockSpec((C, D_B_V), lambda c, blk=blk: (blk(c), vc)),
                     pl.BlockSpec((C, D_B_K), lambda c, blk=blk: (blk(c), 0))]
        out_specs.append(pl.BlockSpec((C, D_B_V), lambda c, blk=blk: (blk(c), 0)))
    return pl.pallas_call(
        _gla_kernel,
        grid=(L // C,),
        in_specs=in_specs,
        out_specs=out_specs,
        out_shape=[jax.ShapeDtypeStruct((L, D_B_V), F32)] * 2,
        scratch_shapes=[pltpu.VMEM((2, H_B, DV_B, DK_B), F32)],
        compiler_params=_cparams(("arbitrary",)),
    )(p_all, p_all, p_all, g_fwd, p_all, p_all, p_all, g_bwd)


def _seg(m, j):
    return m[0:2, j * D_MODEL:(j + 1) * D_MODEL]


def _coef(rows2):
    return jnp.zeros((SUBLANES, D_MODEL), F32).at[0:2].set(rows2)


def _ffn_half(h, m, kidx, gain, w13, w2, lead, n_lat, n_rows):
    shift, scale, gate = _seg(m, 3 * kidx), _seg(m, 3 * kidx + 1), _seg(m, 3 * kidx + 2)
    u = normmod(h, (gain[None, :] * (1.0 + scale))[:, None, :], shift[:, None, :], n_lat, n_rows, BF16)
    act = matmul_swiglu(u, w13, lead)
    return matmul_residual(act, w2, h, _coef(FFN_RES * gate), n_lat, lead)


def _shifted_lora(w, mu):
    return jnp.concatenate([(1.0 - mu[0] - mu[1])[:, None] * w, mu[0][:, None] * w, mu[1][:, None] * w], axis=1)


def _mixer_ab(u, P, n_lat):
    p_all = matmul(u, P['w_big'])
    r, k, v, kk, lw0, lw1, a0, a1, g, la0, la1 = ab_prep(p_all, P, n_lat)
    prep = dict(r=r, k=k, v=v, a=(a0, a1), g=g)
    yf, yb = rwkv_scan(r, (lw0, lw1), k, v, kk, (a0, a1), P['k_a'], n_lat)
    of, ob = gla_scan(p_all, la0, la1, n_lat)
    return ab_out(yf, yb, prep, of, ob, p_all, P, n_lat)


def _rope_tables(n_lat, n_rows):
    rows = n_lat // GRID_W
    rr = jnp.repeat(jnp.arange(rows), GRID_W)
    cc = jnp.tile(jnp.arange(GRID_W), rows)
    inv_freq = ROPE_THETA ** (-jnp.arange(ROPE_PAIRS, dtype=F32) / ROPE_PAIRS)
    ar = rr.astype(F32)[:, None] * inv_freq
    ac = cc.astype(F32)[:, None] * inv_freq
    cos = jnp.concatenate([jnp.cos(ar), jnp.cos(ar), jnp.cos(ac), jnp.cos(ac)], axis=1)
    sin = jnp.concatenate([-jnp.sin(ar), jnp.sin(ar), -jnp.sin(ac), jnp.sin(ac)], axis=1)
    n_ctx = n_rows - n_lat
    cos = jnp.concatenate([cos, jnp.ones((n_ctx, HEAD_C), F32)], axis=0)
    sin = jnp.concatenate([sin, jnp.zeros((n_ctx, HEAD_C), F32)], axis=0)
    return cos, sin


def _mixer_c(u, P, n_lat, need_ctx):
    L = u.shape[0]
    p = matmul(u, P['w_in'], P['lead'])
    q, k, v_ext = attn_prep(p, P['cos'], P['sin'], P['q_gain'], P['k_gain'])
    o_l = flash_gqa(q, k, v_ext, (0, n_lat), (0, L))
    if not need_ctx:
        return o_l
    o_c = flash_gqa(q, k, v_ext, (n_lat, L - n_lat), (n_lat, L - n_lat))
    return jnp.concatenate([o_l, o_c], axis=0)


def kernel(x, c, ctx, c_ctx, ada_down, ada_up, ada_bias, norm_gains, final_gain, ffn_w13, ffn_w2, ab_w_in, ab_w_out, rwkv_mu_rkv, rwkv_mu_lora, rwkv_w1, rwkv_w2, rwkv_w0, rwkv_a1, rwkv_a2, rwkv_a0, rwkv_g1, rwkv_g2, rwkv_k_k, rwkv_k_a, rwkv_r_k, rwkv_ln_w, rwkv_ln_b, gla_a1, gla_a2, gla_ab, gla_norm, attn_w_in, attn_w_out, attn_q_norm, attn_k_norm):
    D = D_MODEL
    n_lat = x.shape[1]
    h = jnp.concatenate([x[0], ctx[0]], axis=0)
    n_rows = h.shape[0]
    bf = lambda t: t.astype(BF16)
    cvec = jnp.zeros((16, D), F32).at[0].set(c[0]).at[1].set(c_ctx)
    cs = bf(cvec * jax.nn.sigmoid(cvec))
    cos, sin = _rope_tables(n_lat, n_rows)
    cat1 = lambda ts: jnp.concatenate(ts, axis=1)
    w13_s, w2_s, ada_down_s, ada_up_s = bf(ffn_w13), bf(ffn_w2), bf(ada_down), bf(ada_up)
    ab_w_out_s, attn_w_in_s, attn_w_out_s = bf(ab_w_out), bf(attn_w_in), bf(attn_w_out)
    for layer in range(DEPTH):
        last = layer == DEPTH - 1
        i = layer // 2
        m = matmul(bf(matmul(cs, ada_down_s, (layer,))), ada_up_s, (layer,)) + ada_bias[layer]
        gains = norm_gains[layer]
        h = _ffn_half(h, m, 0, gains[0], w13_s, w2_s, (layer, 0), n_lat, n_rows)
        u = normmod(h, (gains[1][None] * (1.0 + _seg(m, 4)))[:, None, :], _seg(m, 3)[:, None, :], n_lat, n_rows, BF16)
        if layer % 2 == 0:
            mu = rwkv_mu_lora[i]
            rg = R_GLA_GATE
            lora = cat1([_shifted_lora(cat1([rwkv_w1[i, 0], rwkv_w1[i, 1]]), mu[0]),
                         _shifted_lora(cat1([rwkv_a1[i, 0], rwkv_a1[i, 1]]), mu[1]),
                         _shifted_lora(rwkv_g1[i], mu[2]),
                         gla_a1[i, 0], gla_a1[i, 1]])
            lora = jnp.pad(lora, ((0, 0), (0, LORA_W - lora.shape[1])))
            ga2pad = jnp.stack([jnp.pad(gla_a2[i, d], ((d * rg, LANES - (d + 1) * rg), (0, 0))) for d in range(2)])
            P = dict(
                w_big=bf(cat1([ab_w_in[i], lora])), mu_rkv=rwkv_mu_rkv[i],
                w2=bf(rwkv_w2[i]), w0=rwkv_w0[i], a2=bf(rwkv_a2[i]), a0=rwkv_a0[i], g2=bf(rwkv_g2[i]),
                k_k=rwkv_k_k[i][None], k_a=rwkv_k_a[i][None], r_k=rwkv_r_k[i].reshape(1, D_A),
                ln_w=rwkv_ln_w[i][None], ln_b=rwkv_ln_b[i][None],
                ga2pad=bf(ga2pad), gab=gla_ab[i], g_norm=gla_norm[i][None])
            y = _mixer_ab(u, P, n_lat)
            w_out = ab_w_out_s
        else:
            P = dict(w_in=attn_w_in_s, lead=(i,), q_gain=attn_q_norm[i][None], k_gain=attn_k_norm[i][None],
                     cos=cos, sin=sin)
            y = _mixer_c(u, P, n_lat, not last)
            w_out = attn_w_out_s
        if last:
            n_rows = n_lat
        h = matmul_residual(y, w_out, h, _coef(_seg(m, 5)), n_lat, (i,))
        h = _ffn_half(h, m, 2, gains[2], w13_s, w2_s, (layer, 1), n_lat, n_rows)
    ones2 = jnp.stack([final_gain, final_gain])[:, None, :]
    out = normmod(h, ones2, jnp.zeros((2, 1, D), F32), n_lat, n_lat, F32)
    return out[None]
```
